```python
import jax, jax.numpy as jnp
from jax import lax
import numpy as np

D_MODEL = 1024
BATCH = 8
SEQ = 2048
DEPTH = 1
DEC_BATCH = 128
DEC_SEQ = 4
PAST_LEN = 16384
PAGE_SIZE = 128

D_CONV = D_MODEL
CONV_K = 31
D_RNN = 5 * D_MODEL // 4
RNN_HEADS = 16
RNN_HEAD_DIM = D_RNN // RNN_HEADS
RNN_CONV_K = 4
RG_C = 8.0
D_FF = 4 * D_MODEL
D_PLE = 256
LN_EPS = 1e-5
DN_ALPHA = (2.0 * DEPTH) ** 0.25
DN_BETA = (8.0 * DEPTH) ** -0.25
OFF_A_VAL = 0
OFF_A_GATE = OFF_A_VAL + D_CONV
OFF_B_X = OFF_A_GATE + D_CONV
OFF_B_GATE = OFF_B_X + D_RNN
OFF_G_A = OFF_B_GATE + D_RNN
OFF_G_B = OFF_G_A + D_MODEL
D_IN = OFF_G_B + D_MODEL

kernel_name = 'conformer_rglru_hybrid_step'


def layer_norm(x, g, b):
    xf = x.astype(jnp.float32)
    mu = jnp.mean(xf, axis=-1, keepdims=True)
    var = jnp.mean(jnp.square(xf - mu), axis=-1, keepdims=True)
    y = (xf - mu) * lax.rsqrt(var + LN_EPS) * g.astype(jnp.float32) + b.astype(jnp.float32)
    return y.astype(x.dtype)


def causal_dwconv(ctx, w, b):
    c = ctx.shape[-1]
    out = lax.conv_general_dilated(ctx, w[:, None, :].astype(ctx.dtype), window_strides=(1,),
                                   padding='VALID', dimension_numbers=('NWC', 'WIO', 'NWC'),
                                   feature_group_count=c)
    return out + b


def rg_lru(xc, h0, w_a, b_a, w_x, b_x, lam, reset_first):
    bn, t = xc.shape[0], xc.shape[1]
    xf = xc.astype(jnp.float32)
    xh = xf.reshape(bn, t, RNN_HEADS, RNN_HEAD_DIM)
    r = jax.nn.sigmoid(jnp.einsum('bthi,hij->bthj', xh, w_a.astype(jnp.float32)) + b_a.astype(jnp.float32))
    ig = jax.nn.sigmoid(jnp.einsum('bthi,hij->bthj', xh, w_x.astype(jnp.float32)) + b_x.astype(jnp.float32))
    r = r.reshape(bn, t, D_RNN)
    ig = ig.reshape(bn, t, D_RNN)
    log_a = -RG_C * r * jax.nn.softplus(-lam.astype(jnp.float32))
    a = jnp.exp(log_a)
    mult = jnp.sqrt(-jnp.expm1(2.0 * log_a))
    if reset_first:
        mult = mult.at[:, 0].set(1.0)
        a0 = jnp.zeros_like(a[:, 0])
    else:
        a0 = a[:, 0]
    bterm = mult * ig * xf
    bterm = bterm.at[:, 0].add(a0 * h0.astype(jnp.float32))

    def combine(left, right):
        a1, b1 = left
        a2, b2 = right
        return a1 * a2, a2 * b1 + b2

    _, h = lax.associative_scan(combine, (a, bterm), axis=1)
    return h, h[:, -1]


def trunk_layer(x, p, ctx_a, ctx_b, h0, reset_first,
                w_in, w_dw_a, b_dw_a, ln_a_g, ln_a_b, w_proj_a,
                w_dw_b, b_dw_b, w_rg_a, b_rg_a, w_rg_x, b_rg_x, rg_lam, w_proj_b,
                w_out, ln1_g, ln1_b, w_ff1, w_ff2, w_ple_gate, w_ple_proj, ln2_g, ln2_b):
    z = x @ w_in
    a_val = z[..., OFF_A_VAL:OFF_A_GATE]
    a_gate = z[..., OFF_A_GATE:OFF_B_X]
    b_x = z[..., OFF_B_X:OFF_B_GATE]
    b_gate = z[..., OFF_B_GATE:OFF_G_A]
    g_a = z[..., OFF_G_A:OFF_G_B]
    g_b = z[..., OFF_G_B:]
    u = a_val * jax.nn.sigmoid(a_gate)
    ua = jnp.concatenate([ctx_a, u], axis=1)
    ca = jax.nn.silu(layer_norm(causal_dwconv(ua, w_dw_a, b_dw_a), ln_a_g, ln_a_b))
    y_a = ca @ w_proj_a
    ub = jnp.concatenate([ctx_b, b_x], axis=1)
    cb = causal_dwconv(ub, w_dw_b, b_dw_b)
    hs, h_last = rg_lru(cb, h0, w_rg_a, b_rg_a, w_rg_x, b_rg_x, rg_lam, reset_first)
    y_b = (hs.astype(x.dtype) * jax.nn.gelu(b_gate)) @ w_proj_b
    mix = (jax.nn.sigmoid(g_a) * y_a + jax.nn.sigmoid(g_b) * y_b) @ w_out
    x1 = layer_norm(DN_ALPHA * x + mix, ln1_g, ln1_b)
    ff = jnp.square(jax.nn.relu(x1 @ w_ff1)) @ w_ff2
    ple = jax.nn.sigmoid(x1 @ w_ple_gate) * (p @ w_ple_proj)
    y = layer_norm(DN_ALPHA * x1 + ff + ple, ln2_g, ln2_b)
    return y, ua[:, -(CONV_K - 1):], ub[:, -(RNN_CONV_K - 1):], h_last.astype(h0.dtype)


def setup_inputs(seed: int = 0) -> dict:
    key = jax.random.key(seed)
    ks = jax.random.split(key, 40)
    f32 = jnp.float32

    def nrm(k, shape, scale):
        return jax.random.normal(k, shape, f32) * scale

    u = jax.random.uniform(ks[0], (DEPTH, D_RNN), f32, minval=0.9, maxval=0.999)
    a_init = u ** (1.0 / RG_C)
    rg_lam = jnp.log(a_init) - jnp.log1p(-a_init)
    return {
        'x_prompt': nrm(ks[1], (BATCH, SEQ, D_MODEL), 1.0),
        'x_sample': nrm(ks[2], (DEC_BATCH, DEC_SEQ, D_MODEL), 1.0),
        'state_conv_a': nrm(ks[3], (DEPTH, DEC_BATCH, CONV_K - 1, D_CONV), 0.5),
        'state_conv_b': nrm(ks[4], (DEPTH, DEC_BATCH, RNN_CONV_K - 1, D_RNN), 1.0),
        'state_h': nrm(ks[5], (DEPTH, DEC_BATCH, D_RNN), 0.5),
        'p_prompt': nrm(ks[6], (DEPTH, BATCH, SEQ, D_PLE), 1.0),
        'p_sample': nrm(ks[7], (DEPTH, DEC_BATCH, DEC_SEQ, D_PLE), 1.0),
        'w_in': nrm(ks[8], (DEPTH, D_MODEL, D_IN), D_MODEL ** -0.5),
        'w_dw_a': nrm(ks[9], (DEPTH, CONV_K, D_CONV), CONV_K ** -0.5),
        'b_dw_a': nrm(ks[10], (DEPTH, D_CONV), 0.01),
        'ln_a_g': 1.0 + nrm(ks[11], (DEPTH, D_CONV), 0.02),
        'ln_a_b': nrm(ks[12], (DEPTH, D_CONV), 0.02),
        'w_proj_a': nrm(ks[13], (DEPTH, D_CONV, D_MODEL), DN_BETA * D_CONV ** -0.5),
        'w_dw_b': nrm(ks[14], (DEPTH, RNN_CONV_K, D_RNN), RNN_CONV_K ** -0.5),
        'b_dw_b': nrm(ks[15], (DEPTH, D_RNN), 0.01),
        'w_rg_a': nrm(ks[16], (DEPTH, RNN_HEADS, RNN_HEAD_DIM, RNN_HEAD_DIM), RNN_HEAD_DIM ** -0.5),
        'b_rg_a': nrm(ks[17], (DEPTH, RNN_HEADS, RNN_HEAD_DIM), 0.01),
        'w_rg_x': nrm(ks[18], (DEPTH, RNN_HEADS, RNN_HEAD_DIM, RNN_HEAD_DIM), RNN_HEAD_DIM ** -0.5),
        'b_rg_x': nrm(ks[19], (DEPTH, RNN_HEADS, RNN_HEAD_DIM), 0.01),
        'rg_lam': rg_lam,
        'w_proj_b': nrm(ks[20], (DEPTH, D_RNN, D_MODEL), DN_BETA * D_RNN ** -0.5),
        'w_out': nrm(ks[21], (DEPTH, D_MODEL, D_MODEL), DN_BETA * D_MODEL ** -0.5),
        'ln1_g': 1.0 + nrm(ks[22], (DEPTH, D_MODEL), 0.02),
        'ln1_b': nrm(ks[23], (DEPTH, D_MODEL), 0.02),
        'w_ff1': nrm(ks[24], (DEPTH, D_MODEL, D_FF), D_MODEL ** -0.5),
        'w_ff2': nrm(ks[25], (DEPTH, D_FF, D_MODEL), DN_BETA * D_FF ** -0.5),
        'w_ple_gate': nrm(ks[26], (DEPTH, D_MODEL, D_MODEL), D_MODEL ** -0.5),
        'w_ple_proj': nrm(ks[27], (DEPTH, D_PLE, D_MODEL), DN_BETA * D_PLE ** -0.5),
        'ln2_g': 1.0 + nrm(ks[28], (DEPTH, D_MODEL), 0.02),
        'ln2_b': nrm(ks[29], (DEPTH, D_MODEL), 0.02),
    }


def reference(x_prompt, x_sample, state_conv_a, state_conv_b, state_h, p_prompt, p_sample,
              w_in, w_dw_a, b_dw_a, ln_a_g, ln_a_b, w_proj_a,
              w_dw_b, b_dw_b, w_rg_a, b_rg_a, w_rg_x, b_rg_x, rg_lam, w_proj_b,
              w_out, ln1_g, ln1_b, w_ff1, w_ff2, w_ple_gate, w_ple_proj, ln2_g, ln2_b):
    yp, ys = x_prompt, x_sample
    bp = x_prompt.shape[0]
    ca_p, cb_p, h_p, ca_s, cb_s, h_s = [], [], [], [], [], []
    for i in range(DEPTH):
        lw = (w_in[i], w_dw_a[i], b_dw_a[i], ln_a_g[i], ln_a_b[i], w_proj_a[i],
              w_dw_b[i], b_dw_b[i], w_rg_a[i], b_rg_a[i], w_rg_x[i], b_rg_x[i], rg_lam[i], w_proj_b[i],
              w_out[i], ln1_g[i], ln1_b[i], w_ff1[i], w_ff2[i], w_ple_gate[i], w_ple_proj[i],
              ln2_g[i], ln2_b[i])
        zero_a = jnp.zeros((bp, CONV_K - 1, D_CONV), yp.dtype)
        zero_b = jnp.zeros((bp, RNN_CONV_K - 1, D_RNN), yp.dtype)
        zero_h = jnp.zeros((bp, D_RNN), state_h.dtype)
        yp, na_p, nb_p, nh_p = trunk_layer(yp, p_prompt[i], zero_a, zero_b, zero_h, True, *lw)
        ys, na_s, nb_s, nh_s = trunk_layer(ys, p_sample[i], state_conv_a[i], state_conv_b[i],
                                           state_h[i], False, *lw)
        ca_p.append(na_p)
        cb_p.append(nb_p)
        h_p.append(nh_p)
        ca_s.append(na_s)
        cb_s.append(nb_s)
        h_s.append(nh_s)
    return (yp, ys, jnp.stack(ca_p), jnp.stack(cb_p), jnp.stack(h_p),
            jnp.stack(ca_s), jnp.stack(cb_s), jnp.stack(h_s))
```

```python
import functools
import math

import jax
import jax.numpy as jnp
from jax import lax
from jax.experimental import pallas as pl
from jax.experimental.pallas import tpu as pltpu

D_MODEL = 1024
D_CONV = D_MODEL
CONV_K = 31
D_RNN = 1280
RNN_HEADS = 16
RNN_HEAD_DIM = D_RNN // RNN_HEADS
RNN_CONV_K = 4
RG_C = 8.0
D_FF = 4 * D_MODEL
D_PLE = 256
LN_EPS = 1e-5
DEPTH = 1
DN_ALPHA = (2.0 * DEPTH) ** 0.25

OFF_A_VAL = 0
OFF_A_GATE = OFF_A_VAL + D_CONV
OFF_B_X = OFF_A_GATE + D_CONV
OFF_B_GATE = OFF_B_X + D_RNN
OFF_G_A = OFF_B_GATE + D_RNN
OFF_G_B = OFF_G_A + D_MODEL
D_IN = OFF_G_B + D_MODEL

HALO_A = CONV_K - 1
HALO_B = RNN_CONV_K - 1

V7X_VMEM_LIMIT_BYTES = 56 * 1024 * 1024
CONV_ROWS = 16
FFN_ROWS = 512

BF16 = jnp.bfloat16
F32 = jnp.float32


def _dot(a, b):
    return jnp.dot(a, b, preferred_element_type=F32)


def _layer_norm(v, g, b):
    mu = jnp.mean(v, axis=-1, keepdims=True)
    c = v - mu
    var = jnp.mean(c * c, axis=-1, keepdims=True)
    return c * lax.rsqrt(var + LN_EPS) * g + b


def _softplus(v):
    return jnp.maximum(v, 0.0) + jnp.log1p(jnp.exp(-jnp.abs(v)))


def _mixer_kernel(x_ref, ctxa_ref, ctxb_ref, h0_ref,
                  w_in_ref, w_dwa_ref, b_dwa_ref, lnag_ref, lnab_ref, w_pa_ref,
                  w_dwb_ref, b_dwb_ref, w_ra_ref, b_ra_ref, w_rx_ref, b_rx_ref, lam_ref,
                  w_pb_ref, w_out_ref, ln1g_ref, ln1b_ref,
                  x1_ref, newa_ref, newb_ref, newh_ref,
                  ua_buf, ub_buf, h_scr, ca_scr, a_scr, b_scr, hs_scr,
                  *, n_seq, t_chunk, n_chunks, reset_first):
    s = pl.program_id(1)
    m = n_seq * t_chunk
    ha = HALO_A * n_seq
    hb = HALO_B * n_seq

    @pl.when(s == 0)
    def _load_state():
        ua_buf[0:ha, :] = ctxa_ref[...].reshape(ha, D_CONV)
        ub_buf[0:hb, :] = ctxb_ref[...].reshape(hb, D_RNN)
        h_scr[...] = h0_ref[...]

    x = x_ref[...].reshape(m, D_MODEL)
    xb = x.astype(BF16)

    z = _dot(xb, w_in_ref[:, OFF_A_VAL:OFF_B_X])
    ua_buf[ha:ha + m, :] = z[:, :D_CONV] * jax.nn.sigmoid(z[:, D_CONV:])

    def conv_a_body(i, carry):
        r0 = pl.multiple_of(i * CONV_ROWS, CONV_ROWS)
        acc = jnp.broadcast_to(b_dwa_ref[...], (CONV_ROWS, D_CONV))
        for k in range(CONV_K):
            acc = acc + w_dwa_ref[k:k + 1, :] * ua_buf[pl.ds(r0 + k * n_seq, CONV_ROWS), :]
        y = _layer_norm(acc, lnag_ref[...], lnab_ref[...])
        ca_scr[pl.ds(r0, CONV_ROWS), :] = (y * jax.nn.sigmoid(y)).astype(BF16)
        return carry

    lax.fori_loop(0, m // CONV_ROWS, conv_a_body, 0)
    y_a = _dot(ca_scr[...], w_pa_ref[...])
    g_a = _dot(xb, w_in_ref[:, OFF_G_A:OFF_G_B])
    mix_in = jax.nn.sigmoid(g_a) * y_a

    ub_buf[hb:hb + m, :] = _dot(xb, w_in_ref[:, OFF_B_X:OFF_B_GATE])
    cb = jnp.broadcast_to(b_dwb_ref[...], (m, D_RNN))
    for k in range(RNN_CONV_K):
        cb = cb + w_dwb_ref[k:k + 1, :] * ub_buf[k * n_seq:k * n_seq + m, :]
    cbb = cb.astype(BF16)
    r = jax.nn.sigmoid(_dot(cbb, w_ra_ref[...]) + b_ra_ref[...])
    ig = jax.nn.sigmoid(_dot(cbb, w_rx_ref[...]) + b_rx_ref[...])
    log_a = (-RG_C * _softplus(-lam_ref[...])) * r
    a = jnp.exp(log_a)
    mult = jnp.sqrt(1.0 - a * a)
    if reset_first:
        row = lax.broadcasted_iota(jnp.int32, (m, 1), 0)
        first = jnp.logical_and(row < n_seq, s == 0)
        a = jnp.where(first, 0.0, a)
        mult = jnp.where(first, 1.0, mult)
    a_scr[...] = a
    b_scr[...] = mult * ig * cb

    def scan_body(t, h):
        r0 = pl.multiple_of(t * n_seq, n_seq)
        h = a_scr[pl.ds(r0, n_seq), :] * h + b_scr[pl.ds(r0, n_seq), :]
        hs_scr[pl.ds(r0, n_seq), :] = h
        return h

    h_last = lax.fori_loop(0, t_chunk, scan_body, h_scr[...])
    h_scr[...] = h_last

    b_gate = _dot(xb, w_in_ref[:, OFF_B_GATE:OFF_G_A])
    yb_in = (hs_scr[...] * jax.nn.gelu(b_gate)).astype(BF16)
    y_b = _dot(yb_in, w_pb_ref[...])
    g_b = _dot(xb, w_in_ref[:, OFF_G_B:D_IN])
    mix_in = (mix_in + jax.nn.sigmoid(g_b) * y_b).astype(BF16)

    mix = _dot(mix_in, w_out_ref[...])
    x1 = _layer_norm(DN_ALPHA * x + mix, ln1g_ref[...], ln1b_ref[...])
    x1_ref[...] = x1.reshape(t_chunk, n_seq, D_MODEL)

    @pl.when(s == n_chunks - 1)
    def _emit_state():
        newa_ref[...] = ua_buf[m:m + ha, :].reshape(HALO_A, n_seq, D_CONV)
        newb_ref[...] = ub_buf[m:m + hb, :].reshape(HALO_B, n_seq, D_RNN)
        newh_ref[...] = h_last

    if n_chunks > 1:
        @pl.when(s < n_chunks - 1)
        def _carry():
            ua_buf[0:ha, :] = ua_buf[m:m + ha, :]
            ub_buf[0:hb, :] = ub_buf[m:m + hb, :]


def _vmem_full():
    return pl.BlockSpec(memory_space=pltpu.VMEM)


def _mixer(x_t, ctxa_t, ctxb_t, h0, weights, *, n_seq, t_chunk, reset_first):
    t_total, n_total, _ = x_t.shape
    n_groups = n_total // n_seq
    n_chunks = t_total // t_chunk
    assert n_groups * n_seq == n_total and n_chunks * t_chunk == t_total
    assert n_seq % 8 == 0 and (n_seq * t_chunk) % CONV_ROWS == 0
    assert n_chunks == 1 or t_chunk >= HALO_A
    m = n_seq * t_chunk

    kernel = functools.partial(_mixer_kernel, n_seq=n_seq, t_chunk=t_chunk,
                               n_chunks=n_chunks, reset_first=reset_first)
    in_specs = [
        pl.BlockSpec((t_chunk, n_seq, D_MODEL), lambda g, s: (s, g, 0)),
        pl.BlockSpec((HALO_A, n_seq, D_CONV), lambda g, s: (0, g, 0)),
        pl.BlockSpec((HALO_B, n_seq, D_RNN), lambda g, s: (0, g, 0)),
        pl.BlockSpec((n_seq, D_RNN), lambda g, s: (g, 0)),
    ] + [_vmem_full() for _ in weights]
    out_specs = [
        pl.BlockSpec((t_chunk, n_seq, D_MODEL), lambda g, s: (s, g, 0)),
        pl.BlockSpec((HALO_A, n_seq, D_CONV), lambda g, s: (0, g, 0)),
        pl.BlockSpec((HALO_B, n_seq, D_RNN), lambda g, s: (0, g, 0)),
        pl.BlockSpec((n_seq, D_RNN), lambda g, s: (g, 0)),
    ]
    out_shape = [
        jax.ShapeDtypeStruct((t_total, n_total, D_MODEL), F32),
        jax.ShapeDtypeStruct((HALO_A, n_total, D_CONV), F32),
        jax.ShapeDtypeStruct((HALO_B, n_total, D_RNN), F32),
        jax.ShapeDtypeStruct((n_total, D_RNN), F32),
    ]
    scratch = [
        pltpu.VMEM(((HALO_A + t_chunk) * n_seq, D_CONV), F32),
        pltpu.VMEM(((HALO_B + t_chunk) * n_seq, D_RNN), F32),
        pltpu.VMEM((n_seq, D_RNN), F32),
        pltpu.VMEM((m, D_CONV), BF16),
        pltpu.VMEM((m, D_RNN), F32),
        pltpu.VMEM((m, D_RNN), F32),
        pltpu.VMEM((m, D_RNN), F32),
    ]
    return pl.pallas_call(
        kernel,
        grid=(n_groups, n_chunks),
        in_specs=in_specs,
        out_specs=out_specs,
        out_shape=out_shape,
        scratch_shapes=scratch,
        compiler_params=pltpu.CompilerParams(
            dimension_semantics=("arbitrary", "arbitrary"),
            vmem_limit_bytes=V7X_VMEM_LIMIT_BYTES),
        name="mixer",
    )(x_t, ctxa_t, ctxb_t, h0, *weights)


def _ffn_kernel(x1_ref, p_ref, w1_ref, w2_ref, wg_ref, wp_ref, ln2g_ref, ln2b_ref, y_ref):
    x1 = x1_ref[...]
    x1b = x1.astype(BF16)
    hid = jnp.maximum(_dot(x1b, w1_ref[...]), 0.0)
    ff = _dot((hid * hid).astype(BF16), w2_ref[...])
    ple = jax.nn.sigmoid(_dot(x1b, wg_ref[...])) * _dot(p_ref[...].astype(BF16), wp_ref[...])
    y_ref[...] = _layer_norm(DN_ALPHA * x1 + ff + ple, ln2g_ref[...], ln2b_ref[...])


def _ffn(x1, p, weights):
    n_rows = x1.shape[0]
    rows = min(FFN_ROWS, n_rows)
    assert n_rows % rows == 0
    in_specs = [
        pl.BlockSpec((rows, D_MODEL), lambda i: (i, 0)),
        pl.BlockSpec((rows, D_PLE), lambda i: (i, 0)),
    ] + [_vmem_full() for _ in weights]
    return pl.pallas_call(
        _ffn_kernel,
        grid=(n_rows // rows,),
        in_specs=in_specs,
        out_specs=pl.BlockSpec((rows, D_MODEL), lambda i: (i, 0)),
        out_shape=jax.ShapeDtypeStruct((n_rows, D_MODEL), F32),
        compiler_params=pltpu.CompilerParams(
            dimension_semantics=("arbitrary",),
            vmem_limit_bytes=V7X_VMEM_LIMIT_BYTES),
        name="ffn",
    )(x1, p, *weights)


def _block_diag(w):
    h, i, j = w.shape
    eye = jnp.eye(h, dtype=w.dtype)
    return (w[:, :, None, :] * eye[:, None, :, None]).reshape(h * i, h * j)


def _group(x, p, ctx_a, ctx_b, h0, mixer_w, ffn_w, *, n_seq, t_chunk, reset_first):
    b, t, _ = x.shape
    x_t = jnp.transpose(x, (1, 0, 2))
    p_t = jnp.transpose(p, (1, 0, 2)).reshape(t * b, D_PLE)
    x1_t, newa_t, newb_t, newh = _mixer(
        x_t, jnp.transpose(ctx_a, (1, 0, 2)), jnp.transpose(ctx_b, (1, 0, 2)), h0, mixer_w,
        n_seq=n_seq, t_chunk=t_chunk, reset_first=reset_first)
    y_t = _ffn(x1_t.reshape(t * b, D_MODEL), p_t, ffn_w).reshape(t, b, D_MODEL)
    return (jnp.transpose(y_t, (1, 0, 2)), jnp.transpose(newa_t, (1, 0, 2)),
            jnp.transpose(newb_t, (1, 0, 2)), newh)


def kernel(x_prompt, x_sample, state_conv_a, state_conv_b, state_h, p_prompt, p_sample, w_in, w_dw_a, b_dw_a, ln_a_g, ln_a_b, w_proj_a, w_dw_b, b_dw_b, w_rg_a, b_rg_a, w_rg_x, b_rg_x, rg_lam, w_proj_b, w_out, ln1_g, ln1_b, w_ff1, w_ff2, w_ple_gate, w_ple_proj, ln2_g, ln2_b):
    assert w_in.shape[0] == DEPTH == 1
    row = lambda v: v[0].reshape(1, -1)
    mixer_w = (
        w_in[0].astype(BF16), w_dw_a[0], row(b_dw_a), row(ln_a_g), row(ln_a_b),
        w_proj_a[0].astype(BF16),
        w_dw_b[0], row(b_dw_b),
        _block_diag(w_rg_a[0]).astype(BF16), row(b_rg_a),
        _block_diag(w_rg_x[0]).astype(BF16), row(b_rg_x), row(rg_lam),
        w_proj_b[0].astype(BF16), w_out[0].astype(BF16), row(ln1_g), row(ln1_b),
    )
    ffn_w = (
        w_ff1[0].astype(BF16), w_ff2[0].astype(BF16), w_ple_gate[0].astype(BF16),
        w_ple_proj[0].astype(BF16), row(ln2_g), row(ln2_b),
    )
    bp = x_prompt.shape[0]
    zeros = lambda *shape: jnp.zeros(shape, F32)
    yp, na_p, nb_p, nh_p = _group(
        x_prompt, p_prompt[0], zeros(bp, HALO_A, D_CONV), zeros(bp, HALO_B, D_RNN),
        zeros(bp, D_RNN), mixer_w, ffn_w, n_seq=bp, t_chunk=32, reset_first=True)
    ys, na_s, nb_s, nh_s = _group(
        x_sample, p_sample[0], state_conv_a[0], state_conv_b[0], state_h[0],
        mixer_w, ffn_w, n_seq=16, t_chunk=x_sample.shape[1], reset_first=False)
    return (yp, ys, na_p[None], nb_p[None], nh_p[None], na_s[None], nb_s[None], nh_s[None])
```

```python
import functools

import jax
import jax.numpy as jnp
from jax import lax
from jax.experimental import pallas as pl
from jax.experimental.pallas import tpu as pltpu

D_MODEL = 1024
D_CONV = D_MODEL
CONV_K = 31
D_RNN = 1280
RNN_HEADS = 16
RNN_HEAD_DIM = D_RNN // RNN_HEADS
RNN_CONV_K = 4
RG_C = 8.0
D_FF = 4 * D_MODEL
D_PLE = 256
LN_EPS = 1e-5
DEPTH = 1
DN_ALPHA = (2.0 * DEPTH) ** 0.25

OFF_A_VAL = 0
OFF_A_GATE = OFF_A_VAL + D_CONV
OFF_B_X = OFF_A_GATE + D_CONV
OFF_B_GATE = OFF_B_X + D_RNN
OFF_G_A = OFF_B_GATE + D_RNN
OFF_G_B = OFF_G_A + D_MODEL
D_IN = OFF_G_B + D_MODEL

HALO_A = CONV_K - 1
HALO_B = RNN_CONV_K - 1

V7X_VMEM_LIMIT_BYTES = 56 * 1024 * 1024
V7X_LANES = 128
V7X_MXU_DIM = 256
CONV_ROWS = 16
FFN_ROWS = 512

GATE_COLS = V7X_MXU_DIM
GATE_WIN = 2 * V7X_MXU_DIM
N_GATE_GROUPS = D_RNN // GATE_COLS


def _gate_window_start(j):
    first_head = (j * GATE_COLS) // RNN_HEAD_DIM
    last_head = (j * GATE_COLS + GATE_COLS - 1) // RNN_HEAD_DIM
    start = min((first_head * RNN_HEAD_DIM) // V7X_LANES * V7X_LANES, D_RNN - GATE_WIN)
    assert start <= first_head * RNN_HEAD_DIM
    assert (last_head + 1) * RNN_HEAD_DIM <= start + GATE_WIN
    return start


GATE_STARTS = tuple(_gate_window_start(j) for j in range(N_GATE_GROUPS))

BF16 = jnp.bfloat16
F32 = jnp.float32


def _dot(a, b):
    return jnp.dot(a, b, preferred_element_type=F32)


def _layer_norm(v, g, b):
    mu = jnp.mean(v, axis=-1, keepdims=True)
    c = v - mu
    var = jnp.mean(c * c, axis=-1, keepdims=True)
    return c * lax.rsqrt(var + LN_EPS) * g + b


def _softplus(v):
    return jnp.maximum(v, 0.0) + jnp.log1p(jnp.exp(-jnp.abs(v)))


def _log2(n):
    assert n > 0 and n & (n - 1) == 0, n
    return n.bit_length() - 1


def _row_permutation(m, inner_src, inner_dst):
    i = lax.broadcasted_iota(jnp.int32, (m, m), 0)
    j = lax.broadcasted_iota(jnp.int32, (m, m), 1)
    src = ((i & (inner_dst - 1)) << _log2(inner_src)) + (i >> _log2(inner_dst))
    return jnp.where(j == src, 1.0, 0.0).astype(BF16)


def _mixer_kernel(x_ref, ctxa_ref, ctxb_ref, h0_ref,
                  w_in_ref, w_dwa_ref, b_dwa_ref, lnag_ref, lnab_ref, w_pa_ref,
                  w_dwb_ref, b_dwb_ref, w_gate_ref, b_ra_ref, b_rx_ref, lam_ref,
                  w_pb_ref, w_out_ref, ln1g_ref, ln1b_ref,
                  x1_ref, newa_ref, newb_ref, newh_ref,
                  ua_buf, ub_buf, h_scr, ca_scr, a_scr, b_scr, hs_scr,
                  *, n_seq, t_chunk, n_chunks, reset_first, batch_major):
    s = pl.program_id(1)
    m = n_seq * t_chunk
    ha = HALO_A * n_seq
    hb = HALO_B * n_seq

    @pl.when(s == 0)
    def _load_state():
        ua_buf[0:ha, :] = ctxa_ref[...].reshape(ha, D_CONV)
        ub_buf[0:hb, :] = ctxb_ref[...].reshape(hb, D_RNN)
        h_scr[...] = h0_ref[...]

    x = x_ref[...].reshape(m, D_MODEL)
    xb = x.astype(BF16)
    if batch_major:
        xb = _dot(_row_permutation(m, t_chunk, n_seq), xb).astype(BF16)

    z = _dot(xb, w_in_ref[:, OFF_A_VAL:OFF_B_X])
    ua_buf[ha:ha + m, :] = z[:, :D_CONV] * jax.nn.sigmoid(z[:, D_CONV:])

    for i in range(m // CONV_ROWS):
        r0 = i * CONV_ROWS
        acc = jnp.broadcast_to(b_dwa_ref[...], (CONV_ROWS, D_CONV))
        for k in range(CONV_K):
            acc = acc + w_dwa_ref[k:k + 1, :] * ua_buf[r0 + k * n_seq:r0 + k * n_seq + CONV_ROWS, :]
        y = _layer_norm(acc, lnag_ref[...], lnab_ref[...])
        ca_scr[r0:r0 + CONV_ROWS, :] = (y * jax.nn.sigmoid(y)).astype(BF16)

    y_a = _dot(ca_scr[...], w_pa_ref[...])
    g_a = _dot(xb, w_in_ref[:, OFF_G_A:OFF_G_B])
    mix_in = jax.nn.sigmoid(g_a) * y_a

    ub_buf[hb:hb + m, :] = _dot(xb, w_in_ref[:, OFF_B_X:OFF_B_GATE])
    cb = jnp.broadcast_to(b_dwb_ref[...], (m, D_RNN))
    for k in range(RNN_CONV_K):
        cb = cb + w_dwb_ref[k:k + 1, :] * ub_buf[k * n_seq:k * n_seq + m, :]
    cbb = cb.astype(BF16)
    log_a_scale = -RG_C * _softplus(-lam_ref[...])
    if reset_first:
        row = lax.broadcasted_iota(jnp.int32, (m, 1), 0)
        first = jnp.logical_and(row < n_seq, s == 0)
    for j in range(N_GATE_GROUPS):
        c0, c1 = j * GATE_COLS, (j + 1) * GATE_COLS
        gates = _dot(cbb[:, GATE_STARTS[j]:GATE_STARTS[j] + GATE_WIN], w_gate_ref[j])
        r = jax.nn.sigmoid(gates[:, :GATE_COLS] + b_ra_ref[:, c0:c1])
        ig = jax.nn.sigmoid(gates[:, GATE_COLS:] + b_rx_ref[:, c0:c1])
        a = jnp.exp(log_a_scale[:, c0:c1] * r)
        mult = jnp.sqrt(1.0 - a * a)
        if reset_first:
            a = jnp.where(first, 0.0, a)
            mult = jnp.where(first, 1.0, mult)
        a_scr[:, c0:c1] = a
        b_scr[:, c0:c1] = mult * ig * cb[:, c0:c1]

    def scan_body(t, h):
        r0 = pl.multiple_of(t * n_seq, n_seq)
        h = a_scr[pl.ds(r0, n_seq), :] * h + b_scr[pl.ds(r0, n_seq), :]
        hs_scr[pl.ds(r0, n_seq), :] = h
        return h

    h_last = lax.fori_loop(0, t_chunk, scan_body, h_scr[...])
    h_scr[...] = h_last

    b_gate = _dot(xb, w_in_ref[:, OFF_B_GATE:OFF_G_A])
    yb_in = (hs_scr[...] * jax.nn.gelu(b_gate)).astype(BF16)
    y_b = _dot(yb_in, w_pb_ref[...])
    g_b = _dot(xb, w_in_ref[:, OFF_G_B:D_IN])
    mix_in = (mix_in + jax.nn.sigmoid(g_b) * y_b).astype(BF16)
    if batch_major:
        mix_in = _dot(_row_permutation(m, n_seq, t_chunk), mix_in).astype(BF16)

    mix = _dot(mix_in, w_out_ref[...])
    x1 = _layer_norm(DN_ALPHA * x + mix, ln1g_ref[...], ln1b_ref[...])
    x1_ref[...] = x1.reshape(x1_ref.shape)

    @pl.when(s == n_chunks - 1)
    def _emit_state():
        newa_ref[...] = ua_buf[m:m + ha, :].reshape(HALO_A, n_seq, D_CONV)
        newb_ref[...] = ub_buf[m:m + hb, :].reshape(HALO_B, n_seq, D_RNN)
        newh_ref[...] = h_last

    if n_chunks > 1:
        @pl.when(s < n_chunks - 1)
        def _carry():
            ua_buf[0:ha, :] = ua_buf[m:m + ha, :]
            ub_buf[0:hb, :] = ub_buf[m:m + hb, :]


def _vmem_full():
    return pl.BlockSpec(memory_space=pltpu.VMEM)


def _mixer(x, ctxa_t, ctxb_t, h0, weights, *, n_seq, t_chunk, reset_first, batch_major):
    if batch_major:
        n_total, t_total, _ = x.shape
        x_block = (n_seq, t_chunk, D_MODEL)
        x_index = lambda g, s: (g, s, 0)
    else:
        t_total, n_total, _ = x.shape
        x_block = (t_chunk, n_seq, D_MODEL)
        x_index = lambda g, s: (s, g, 0)
    n_groups = n_total // n_seq
    n_chunks = t_total // t_chunk
    assert n_groups * n_seq == n_total and n_chunks * t_chunk == t_total
    assert n_seq % 8 == 0 and (n_seq * t_chunk) % CONV_ROWS == 0
    assert n_chunks == 1 or t_chunk >= HALO_A
    assert not batch_major or t_chunk % 16 == 0
    m = n_seq * t_chunk

    kernel = functools.partial(_mixer_kernel, n_seq=n_seq, t_chunk=t_chunk, n_chunks=n_chunks,
                               reset_first=reset_first, batch_major=batch_major)
    state_specs = [
        pl.BlockSpec((HALO_A, n_seq, D_CONV), lambda g, s: (0, g, 0)),
        pl.BlockSpec((HALO_B, n_seq, D_RNN), lambda g, s: (0, g, 0)),
        pl.BlockSpec((n_seq, D_RNN), lambda g, s: (g, 0)),
    ]
    in_specs = [pl.BlockSpec(x_block, x_index)] + state_specs + [_vmem_full() for _ in weights]
    out_specs = [pl.BlockSpec(x_block, x_index)] + state_specs
    out_shape = [
        jax.ShapeDtypeStruct(x.shape, F32),
        jax.ShapeDtypeStruct((HALO_A, n_total, D_CONV), F32),
        jax.ShapeDtypeStruct((HALO_B, n_total, D_RNN), F32),
        jax.ShapeDtypeStruct((n_total, D_RNN), F32),
    ]
    scratch = [
        pltpu.VMEM(((HALO_A + t_chunk) * n_seq, D_CONV), F32),
        pltpu.VMEM(((HALO_B + t_chunk) * n_seq, D_RNN), F32),
        pltpu.VMEM((n_seq, D_RNN), F32),
        pltpu.VMEM((m, D_CONV), BF16),
        pltpu.VMEM((m, D_RNN), F32),
        pltpu.VMEM((m, D_RNN), F32),
        pltpu.VMEM((m, D_RNN), F32),
    ]
    return pl.pallas_call(
        kernel,
        grid=(n_groups, n_chunks),
        in_specs=in_specs,
        out_specs=out_specs,
        out_shape=out_shape,
        scratch_shapes=scratch,
        compiler_params=pltpu.CompilerParams(
            dimension_semantics=("arbitrary", "arbitrary"),
            vmem_limit_bytes=V7X_VMEM_LIMIT_BYTES),
        name="mixer",
    )(x, ctxa_t, ctxb_t, h0, *weights)


def _ffn_kernel(x1_ref, p_ref, w1_ref, w2_ref, wg_ref, wp_ref, ln2g_ref, ln2b_ref, y_ref):
    x1 = x1_ref[...]
    x1b = x1.astype(BF16)
    hid = jnp.maximum(_dot(x1b, w1_ref[...]), 0.0)
    ff = _dot((hid * hid).astype(BF16), w2_ref[...])
    ple = jax.nn.sigmoid(_dot(x1b, wg_ref[...])) * _dot(p_ref[...].astype(BF16), wp_ref[...])
    y_ref[...] = _layer_norm(DN_ALPHA * x1 + ff + ple, ln2g_ref[...], ln2b_ref[...])


def _ffn(x1, p, weights):
    n_rows = x1.shape[0]
    rows = min(FFN_ROWS, n_rows)
    assert n_rows % rows == 0
    in_specs = [
        pl.BlockSpec((rows, D_MODEL), lambda i: (i, 0)),
        pl.BlockSpec((rows, D_PLE), lambda i: (i, 0)),
    ] + [_vmem_full() for _ in weights]
    return pl.pallas_call(
        _ffn_kernel,
        grid=(n_rows // rows,),
        in_specs=in_specs,
        out_specs=pl.BlockSpec((rows, D_MODEL), lambda i: (i, 0)),
        out_shape=jax.ShapeDtypeStruct((n_rows, D_MODEL), F32),
        compiler_params=pltpu.CompilerParams(
            dimension_semantics=("arbitrary",),
            vmem_limit_bytes=V7X_VMEM_LIMIT_BYTES),
        name="ffn",
    )(x1, p, *weights)


def _block_diag(w):
    h, i, j = w.shape
    eye = jnp.eye(h, dtype=w.dtype)
    return (w[:, :, None, :] * eye[:, None, :, None]).reshape(h * i, h * j)


def _gate_windows(w_rg_a, w_rg_x):
    dense_a, dense_x = _block_diag(w_rg_a), _block_diag(w_rg_x)
    groups = []
    for j, start in enumerate(GATE_STARTS):
        rows = slice(start, start + GATE_WIN)
        cols = slice(j * GATE_COLS, (j + 1) * GATE_COLS)
        groups.append(jnp.concatenate([dense_a[rows, cols], dense_x[rows, cols]], axis=1))
    return jnp.stack(groups).astype(BF16)


def _time_major(v):
    return jnp.transpose(v, (1, 0, 2))


def kernel(x_prompt, x_sample, state_conv_a, state_conv_b, state_h, p_prompt, p_sample, w_in, w_dw_a, b_dw_a, ln_a_g, ln_a_b, w_proj_a, w_dw_b, b_dw_b, w_rg_a, b_rg_a, w_rg_x, b_rg_x, rg_lam, w_proj_b, w_out, ln1_g, ln1_b, w_ff1, w_ff2, w_ple_gate, w_ple_proj, ln2_g, ln2_b):
    assert w_in.shape[0] == DEPTH == 1
    row = lambda v: v[0].reshape(1, -1)
    mixer_w = (
        w_in[0].astype(BF16), w_dw_a[0], row(b_dw_a), row(ln_a_g), row(ln_a_b),
        w_proj_a[0].astype(BF16),
        w_dw_b[0], row(b_dw_b),
        _gate_windows(w_rg_a[0], w_rg_x[0]), row(b_rg_a), row(b_rg_x), row(rg_lam),
        w_proj_b[0].astype(BF16), w_out[0].astype(BF16), row(ln1_g), row(ln1_b),
    )
    ffn_w = (
        w_ff1[0].astype(BF16), w_ff2[0].astype(BF16), w_ple_gate[0].astype(BF16),
        w_ple_proj[0].astype(BF16), row(ln2_g), row(ln2_b),
    )

    bp, tp, _ = x_prompt.shape
    x1_p, na_p, nb_p, nh_p = _mixer(
        x_prompt, jnp.zeros((HALO_A, bp, D_CONV), F32), jnp.zeros((HALO_B, bp, D_RNN), F32),
        jnp.zeros((bp, D_RNN), F32), mixer_w,
        n_seq=bp, t_chunk=32, reset_first=True, batch_major=True)
    y_p = _ffn(x1_p.reshape(bp * tp, D_MODEL), p_prompt[0].reshape(bp * tp, D_PLE), ffn_w)

    bs, ts, _ = x_sample.shape
    x1_s, na_s, nb_s, nh_s = _mixer(
        _time_major(x_sample), _time_major(state_conv_a[0]), _time_major(state_conv_b[0]),
        state_h[0], mixer_w, n_seq=16, t_chunk=ts, reset_first=False, batch_major=False)
    y_s = _ffn(x1_s.reshape(ts * bs, D_MODEL), _time_major(p_sample[0]).reshape(ts * bs, D_PLE),
               ffn_w)

    return (y_p.reshape(bp, tp, D_MODEL), _time_major(y_s.reshape(ts, bs, D_MODEL)),
            _time_major(na_p)[None], _time_major(nb_p)[None], nh_p[None],
            _time_major(na_s)[None], _time_major(nb_s)[None], nh_s[None])
```

```python
import functools

import jax
import jax.numpy as jnp
from jax import lax
from jax.experimental import pallas as pl
from jax.experimental.pallas import tpu as pltpu

D_MODEL = 1024
D_CONV = D_MODEL
CONV_K = 31
D_RNN = 1280
RNN_HEADS = 16
RNN_HEAD_DIM = D_RNN // RNN_HEADS
RNN_CONV_K = 4
RG_C = 8.0
D_FF = 4 * D_MODEL
D_PLE = 256
LN_EPS = 1e-5
DEPTH = 1
DN_ALPHA = (2.0 * DEPTH) ** 0.25

OFF_A_VAL = 0
OFF_A_GATE = OFF_A_VAL + D_CONV
OFF_B_X = OFF_A_GATE + D_CONV
OFF_B_GATE = OFF_B_X + D_RNN
OFF_G_A = OFF_B_GATE + D_RNN
OFF_G_B = OFF_G_A + D_MODEL
D_IN = OFF_G_B + D_MODEL

HALO_A = CONV_K - 1
HALO_B = RNN_CONV_K - 1

V7X_VMEM_LIMIT_BYTES = 56 * 1024 * 1024
V7X_LANES = 128
V7X_SUBLANES = 8
V7X_MXU_DIM = 256
CONV_ROWS = 32
FFN_ROWS = 512

GATE_COLS = V7X_MXU_DIM
GATE_WIN = 2 * V7X_MXU_DIM
N_GATE_GROUPS = D_RNN // GATE_COLS


def _gate_window_start(j):
    first_head = (j * GATE_COLS) // RNN_HEAD_DIM
    last_head = (j * GATE_COLS + GATE_COLS - 1) // RNN_HEAD_DIM
    start = min((first_head * RNN_HEAD_DIM) // V7X_LANES * V7X_LANES, D_RNN - GATE_WIN)
    assert start <= first_head * RNN_HEAD_DIM
    assert (last_head + 1) * RNN_HEAD_DIM <= start + GATE_WIN
    return start


GATE_STARTS = tuple(_gate_window_start(j) for j in range(N_GATE_GROUPS))

REST_PIECE = 2 * V7X_MXU_DIM
N_REST_PIECES = (D_IN - OFF_B_X) // REST_PIECE
REST_B_GATE = OFF_B_GATE - OFF_B_X
REST_G_A = OFF_G_A - OFF_B_X
REST_G_B = OFF_G_B - OFF_B_X
REST_END = D_IN - OFF_B_X
assert N_REST_PIECES * REST_PIECE == REST_END

BF16 = jnp.bfloat16
F32 = jnp.float32


def _dot(a, b):
    return jnp.dot(a, b, preferred_element_type=F32)


def _layer_norm(v, g, b):
    mu = jnp.mean(v, axis=-1, keepdims=True)
    c = v - mu
    var = jnp.mean(c * c, axis=-1, keepdims=True)
    return c * lax.rsqrt(var + LN_EPS) * g + b


def _softplus(v):
    return jnp.maximum(v, 0.0) + jnp.log1p(jnp.exp(-jnp.abs(v)))


def _log2(n):
    assert n > 0 and n & (n - 1) == 0, n
    return n.bit_length() - 1


def _row_permutation(m, inner_src, inner_dst):
    i = lax.broadcasted_iota(jnp.int32, (m, m), 0)
    j = lax.broadcasted_iota(jnp.int32, (m, m), 1)
    src = ((i & (inner_dst - 1)) << _log2(inner_src)) + (i >> _log2(inner_dst))
    return jnp.where(j == src, 1.0, 0.0).astype(BF16)


def _mixer_kernel(x_ref, ctxa_ref, ctxb_ref, h0_ref,
                  w_glu_ref, w_rest_ref, w_dwa_ref, b_dwa_ref, lnag_ref, lnab_ref, w_pa_ref,
                  w_dwb_ref, b_dwb_ref, w_gate_ref, b_ra_ref, b_rx_ref, lam_ref,
                  w_pb_ref, w_out_ref, ln1g_ref, ln1b_ref,
                  x1_ref, newa_ref, newb_ref, newh_ref,
                  ua_buf, ub_halo, h_scr, xb_scr, conv_scr, ca_scr, cbb_scr, a_scr, b_scr, hs_scr,
                  rest_scr,
                  *, n_seq, t_chunk, n_chunks, reset_first, batch_major):
    s = pl.program_id(1)
    m = n_seq * t_chunk
    ha = HALO_A * n_seq
    hb = HALO_B * n_seq

    @pl.when(s == 0)
    def _load_state():
        ua_buf[0:ha, :] = ctxa_ref[...].reshape(ha, D_CONV)
        ub_halo[...] = ctxb_ref[...].reshape(hb, D_RNN)
        h_scr[...] = h0_ref[...]

    x = x_ref[...].reshape(m, D_MODEL)
    xb = x.astype(BF16)
    if batch_major:
        xb = _dot(_row_permutation(m, t_chunk, n_seq), xb).astype(BF16)
    xb_scr[...] = xb

    z = _dot(xb, w_glu_ref[...])
    ua_buf[ha:ha + m, :] = z[:, :D_CONV] * jax.nn.sigmoid(z[:, D_CONV:])

    def project_rest(p):
        rest_scr[p] = _dot(xb_scr[...], w_rest_ref[p])

    def rest_cols(c0, c1):
        parts = []
        while c0 < c1:
            p, off = divmod(c0, REST_PIECE)
            width = min(c1 - c0, REST_PIECE - off)
            parts.append(rest_scr[p, :, off:off + width])
            c0 += width
        return parts[0] if len(parts) == 1 else jnp.concatenate(parts, axis=1)

    def conv_rows(r0):
        for lane0 in range(0, D_CONV, V7X_LANES):
            lanes = slice(lane0, lane0 + V7X_LANES)
            loaded = {}

            def context(off):
                if off not in loaded:
                    loaded[off] = ua_buf[pl.ds(r0 + off, V7X_SUBLANES), lanes]
                return loaded[off]

            groups = range(0, CONV_ROWS, V7X_SUBLANES)
            accs = {g: jnp.broadcast_to(b_dwa_ref[:, lanes], (V7X_SUBLANES, V7X_LANES))
                    for g in groups}
            for k in range(CONV_K):
                w_k = w_dwa_ref[k, :, lanes]
                for g in groups:
                    accs[g] = accs[g] + w_k * context(g + k * n_seq)
            for g in groups:
                conv_scr[pl.ds(r0 + g, V7X_SUBLANES), lanes] = accs[g]

    def norm_act_rows(r0):
        y = _layer_norm(conv_scr[pl.ds(r0, CONV_ROWS), :], lnag_ref[...], lnab_ref[...])
        ca_scr[pl.ds(r0, CONV_ROWS), :] = (y * jax.nn.sigmoid(y)).astype(BF16)

    n_conv_steps = m // CONV_ROWS
    pieces_per_step = max(1, N_REST_PIECES // n_conv_steps)
    n_pre = N_REST_PIECES - pieces_per_step * (n_conv_steps - 1)
    assert n_pre >= 0
    for p in range(n_pre):
        project_rest(p)
    conv_rows(0)

    def conv_step(i, carry):
        r0 = pl.multiple_of(i * CONV_ROWS, CONV_ROWS)
        norm_act_rows(pl.multiple_of(r0 - CONV_ROWS, CONV_ROWS))
        for q in range(pieces_per_step):
            project_rest(n_pre + (i - 1) * pieces_per_step + q)
        conv_rows(r0)
        return carry

    lax.fori_loop(1, n_conv_steps, conv_step, 0)
    norm_act_rows((n_conv_steps - 1) * CONV_ROWS)

    y_a = _dot(ca_scr[...], w_pa_ref[...])
    mix_in = jax.nn.sigmoid(rest_cols(REST_G_A, REST_G_B)) * y_a

    for c0 in range(0, D_RNN, GATE_COLS):
        c1 = c0 + GATE_COLS
        ub = jnp.concatenate([ub_halo[:, c0:c1], rest_cols(c0, c1)], axis=0)
        cb = jnp.broadcast_to(b_dwb_ref[:, c0:c1], (m, GATE_COLS))
        for k in range(RNN_CONV_K):
            cb = cb + w_dwb_ref[k:k + 1, c0:c1] * ub[k * n_seq:k * n_seq + m]
        b_scr[:, c0:c1] = cb
        cbb_scr[:, c0:c1] = cb.astype(BF16)
        ub_halo[:, c0:c1] = ub[m:m + hb]

    log_a_scale = -RG_C * _softplus(-lam_ref[...])
    if reset_first:
        row = lax.broadcasted_iota(jnp.int32, (m, 1), 0)
        first = jnp.logical_and(row < n_seq, s == 0)
    for j in range(N_GATE_GROUPS):
        c0, c1 = j * GATE_COLS, (j + 1) * GATE_COLS
        gates = _dot(cbb_scr[:, GATE_STARTS[j]:GATE_STARTS[j] + GATE_WIN], w_gate_ref[j])
        r = jax.nn.sigmoid(gates[:, :GATE_COLS] + b_ra_ref[:, c0:c1])
        ig = jax.nn.sigmoid(gates[:, GATE_COLS:] + b_rx_ref[:, c0:c1])
        a = jnp.exp(log_a_scale[:, c0:c1] * r)
        mult = jnp.sqrt(1.0 - a * a)
        if reset_first:
            a = jnp.where(first, 0.0, a)
            mult = jnp.where(first, 1.0, mult)
        a_scr[:, c0:c1] = a
        b_scr[:, c0:c1] = mult * ig * b_scr[:, c0:c1]

    def scan_body(t, h):
        r0 = pl.multiple_of(t * n_seq, n_seq)
        h = a_scr[pl.ds(r0, n_seq), :] * h + b_scr[pl.ds(r0, n_seq), :]
        hs_scr[pl.ds(r0, n_seq), :] = h
        return h

    h_last = lax.fori_loop(0, t_chunk, scan_body, h_scr[...])
    h_scr[...] = h_last

    yb_in = (hs_scr[...] * jax.nn.gelu(rest_cols(REST_B_GATE, REST_G_A))).astype(BF16)
    y_b = _dot(yb_in, w_pb_ref[...])
    mix_in = (mix_in + jax.nn.sigmoid(rest_cols(REST_G_B, REST_END)) * y_b).astype(BF16)
    if batch_major:
        mix_in = _dot(_row_permutation(m, n_seq, t_chunk), mix_in).astype(BF16)

    mix = _dot(mix_in, w_out_ref[...])
    x1 = _layer_norm(DN_ALPHA * x + mix, ln1g_ref[...], ln1b_ref[...])
    x1_ref[...] = x1.reshape(x1_ref.shape)

    @pl.when(s == n_chunks - 1)
    def _emit_state():
        newa_ref[...] = ua_buf[m:m + ha, :].reshape(HALO_A, n_seq, D_CONV)
        newb_ref[...] = ub_halo[...].reshape(HALO_B, n_seq, D_RNN)
        newh_ref[...] = h_last

    if n_chunks > 1:
        @pl.when(s < n_chunks - 1)
        def _carry():
            ua_buf[0:ha, :] = ua_buf[m:m + ha, :]


def _vmem_full():
    return pl.BlockSpec(memory_space=pltpu.VMEM)


def _mixer(x, ctxa_t, ctxb_t, h0, weights, *, n_seq, t_chunk, reset_first, batch_major):
    if batch_major:
        n_total, t_total, _ = x.shape
        x_block = (n_seq, t_chunk, D_MODEL)
        x_index = lambda g, s: (g, s, 0)
    else:
        t_total, n_total, _ = x.shape
        x_block = (t_chunk, n_seq, D_MODEL)
        x_index = lambda g, s: (s, g, 0)
    n_groups = n_total // n_seq
    n_chunks = t_total // t_chunk
    assert n_groups * n_seq == n_total and n_chunks * t_chunk == t_total
    assert n_seq % V7X_SUBLANES == 0 and (n_seq * t_chunk) % CONV_ROWS == 0
    assert n_chunks == 1 or t_chunk >= HALO_A
    assert not batch_major or t_chunk % 16 == 0
    m = n_seq * t_chunk

    kernel = functools.partial(_mixer_kernel, n_seq=n_seq, t_chunk=t_chunk, n_chunks=n_chunks,
                               reset_first=reset_first, batch_major=batch_major)
    state_specs = [
        pl.BlockSpec((HALO_A, n_seq, D_CONV), lambda g, s: (0, g, 0)),
        pl.BlockSpec((HALO_B, n_seq, D_RNN), lambda g, s: (0, g, 0)),
        pl.BlockSpec((n_seq, D_RNN), lambda g, s: (g, 0)),
    ]
    in_specs = [pl.BlockSpec(x_block, x_index)] + state_specs + [_vmem_full() for _ in weights]
    out_specs = [pl.BlockSpec(x_block, x_index)] + state_specs
    out_shape = [
        jax.ShapeDtypeStruct(x.shape, F32),
        jax.ShapeDtypeStruct((HALO_A, n_total, D_CONV), F32),
        jax.ShapeDtypeStruct((HALO_B, n_total, D_RNN), F32),
        jax.ShapeDtypeStruct((n_total, D_RNN), F32),
    ]
    scratch = [
        pltpu.VMEM(((HALO_A + t_chunk) * n_seq, D_CONV), F32),
        pltpu.VMEM((HALO_B * n_seq, D_RNN), F32),
        pltpu.VMEM((n_seq, D_RNN), F32),
        pltpu.VMEM((m, D_MODEL), BF16),
        pltpu.VMEM((m, D_CONV), F32),
        pltpu.VMEM((m, D_CONV), BF16),
        pltpu.VMEM((m, D_RNN), BF16),
        pltpu.VMEM((m, D_RNN), F32),
        pltpu.VMEM((m, D_RNN), F32),
        pltpu.VMEM((m, D_RNN), F32),
        pltpu.VMEM((N_REST_PIECES, m, REST_PIECE), F32),
    ]
    return pl.pallas_call(
        kernel,
        grid=(n_groups, n_chunks),
        in_specs=in_specs,
        out_specs=out_specs,
        out_shape=out_shape,
        scratch_shapes=scratch,
        compiler_params=pltpu.CompilerParams(
            dimension_semantics=("arbitrary", "arbitrary"),
            vmem_limit_bytes=V7X_VMEM_LIMIT_BYTES),
        name="mixer",
    )(x, ctxa_t, ctxb_t, h0, *weights)


def _ffn_kernel(x1_ref, p_ref, w1_ref, w2_ref, wg_ref, wp_ref, ln2g_ref, ln2b_ref, y_ref):
    x1 = x1_ref[...]
    x1b = x1.astype(BF16)
    hid = jnp.maximum(_dot(x1b, w1_ref[...]), 0.0)
    ff = _dot((hid * hid).astype(BF16), w2_ref[...])
    ple = jax.nn.sigmoid(_dot(x1b, wg_ref[...])) * _dot(p_ref[...].astype(BF16), wp_ref[...])
    y_ref[...] = _layer_norm(DN_ALPHA * x1 + ff + ple, ln2g_ref[...], ln2b_ref[...])


def _ffn(x1, p, weights):
    n_rows = x1.shape[0]
    rows = min(FFN_ROWS, n_rows)
    assert n_rows % rows == 0
    in_specs = [
        pl.BlockSpec((rows, D_MODEL), lambda i: (i, 0)),
        pl.BlockSpec((rows, D_PLE), lambda i: (i, 0)),
    ] + [_vmem_full() for _ in weights]
    return pl.pallas_call(
        _ffn_kernel,
        grid=(n_rows // rows,),
        in_specs=in_specs,
        out_specs=pl.BlockSpec((rows, D_MODEL), lambda i: (i, 0)),
        out_shape=jax.ShapeDtypeStruct((n_rows, D_MODEL), F32),
        compiler_params=pltpu.CompilerParams(
            dimension_semantics=("arbitrary",),
            vmem_limit_bytes=V7X_VMEM_LIMIT_BYTES),
        name="ffn",
    )(x1, p, *weights)


def _block_diag(w):
    h, i, j = w.shape
    eye = jnp.eye(h, dtype=w.dtype)
    return (w[:, :, None, :] * eye[:, None, :, None]).reshape(h * i, h * j)


def _gate_windows(w_rg_a, w_rg_x):
    dense_a, dense_x = _block_diag(w_rg_a), _block_diag(w_rg_x)
    groups = []
    for j, start in enumerate(GATE_STARTS):
        rows = slice(start, start + GATE_WIN)
        cols = slice(j * GATE_COLS, (j + 1) * GATE_COLS)
        groups.append(jnp.concatenate([dense_a[rows, cols], dense_x[rows, cols]], axis=1))
    return jnp.stack(groups).astype(BF16)


def _time_major(v):
    return jnp.transpose(v, (1, 0, 2))


def kernel(x_prompt, x_sample, state_conv_a, state_conv_b, state_h, p_prompt, p_sample, w_in, w_dw_a, b_dw_a, ln_a_g, ln_a_b, w_proj_a, w_dw_b, b_dw_b, w_rg_a, b_rg_a, w_rg_x, b_rg_x, rg_lam, w_proj_b, w_out, ln1_g, ln1_b, w_ff1, w_ff2, w_ple_gate, w_ple_proj, ln2_g, ln2_b):
    assert w_in.shape[0] == DEPTH == 1
    row = lambda v: v[0].reshape(1, -1)
    w_rest = w_in[0][:, OFF_B_X:].astype(BF16).reshape(D_MODEL, N_REST_PIECES, REST_PIECE)
    mixer_w = (
        w_in[0][:, :OFF_B_X].astype(BF16),
        jnp.transpose(w_rest, (1, 0, 2)),
        jnp.broadcast_to(w_dw_a[0][:, None, :], (CONV_K, V7X_SUBLANES, D_CONV)),
        row(b_dw_a), row(ln_a_g), row(ln_a_b),
        w_proj_a[0].astype(BF16),
        w_dw_b[0], row(b_dw_b),
        _gate_windows(w_rg_a[0], w_rg_x[0]), row(b_rg_a), row(b_rg_x), row(rg_lam),
        w_proj_b[0].astype(BF16), w_out[0].astype(BF16), row(ln1_g), row(ln1_b),
    )
    ffn_w = (
        w_ff1[0].astype(BF16), w_ff2[0].astype(BF16), w_ple_gate[0].astype(BF16),
        w_ple_proj[0].astype(BF16), row(ln2_g), row(ln2_b),
    )

    bp, tp, _ = x_prompt.shape
    x1_p, na_p, nb_p, nh_p = _mixer(
        x_prompt, jnp.zeros((HALO_A, bp, D_CONV), F32), jnp.zeros((HALO_B, bp, D_RNN), F32),
        jnp.zeros((bp, D_RNN), F32), mixer_w,
        n_seq=bp, t_chunk=32, reset_first=True, batch_major=True)
    y_p = _ffn(x1_p.reshape(bp * tp, D_MODEL), p_prompt[0].reshape(bp * tp, D_PLE), ffn_w)

    bs, ts, _ = x_sample.shape
    x1_s, na_s, nb_s, nh_s = _mixer(
        _time_major(x_sample), _time_major(state_conv_a[0]), _time_major(state_conv_b[0]),
        state_h[0], mixer_w, n_seq=16, t_chunk=ts, reset_first=False, batch_major=False)
    y_s = _ffn(x1_s.reshape(ts * bs, D_MODEL), _time_major(p_sample[0]).reshape(ts * bs, D_PLE),
               ffn_w)

    return (y_p.reshape(bp, tp, D_MODEL), _time_major(y_s.reshape(ts, bs, D_MODEL)),
            _time_major(na_p)[None], _time_major(nb_p)[None], nh_p[None],
            _time_major(na_s)[None], _time_major(nb_s)[None], nh_s[None])
```

```python
import functools

import jax
import jax.numpy as jnp
from jax import lax
from jax.experimental import pallas as pl
from jax.experimental.pallas import tpu as pltpu

D_MODEL = 1024
D_CONV = D_MODEL
CONV_K = 31
D_RNN = 1280
RNN_HEADS = 16
RNN_HEAD_DIM = D_RNN // RNN_HEADS
RNN_CONV_K = 4
RG_C = 8.0
D_FF = 4 * D_MODEL
D_PLE = 256
LN_EPS = 1e-5
DEPTH = 1
DN_ALPHA = (2.0 * DEPTH) ** 0.25

OFF_A_VAL = 0
OFF_A_GATE = OFF_A_VAL + D_CONV
OFF_B_X = OFF_A_GATE + D_CONV
OFF_B_GATE = OFF_B_X + D_RNN
OFF_G_A = OFF_B_GATE + D_RNN
OFF_G_B = OFF_G_A + D_MODEL
D_IN = OFF_G_B + D_MODEL

HALO_A = CONV_K - 1
HALO_B = RNN_CONV_K - 1

V7X_VMEM_LIMIT_BYTES = 56 * 1024 * 1024
V7X_LANES = 128
V7X_SUBLANES = 8
V7X_MXU_DIM = 256
CONV_ROWS = 16
GLU_BLOCKS = 2
FFN_ROWS = 512

GATE_COLS = V7X_MXU_DIM
GATE_WIN = 2 * V7X_MXU_DIM
N_GATE_GROUPS = D_RNN // GATE_COLS


def _gate_window_start(j):
    first_head = (j * GATE_COLS) // RNN_HEAD_DIM
    last_head = (j * GATE_COLS + GATE_COLS - 1) // RNN_HEAD_DIM
    start = min((first_head * RNN_HEAD_DIM) // V7X_LANES * V7X_LANES, D_RNN - GATE_WIN)
    assert start <= first_head * RNN_HEAD_DIM
    assert (last_head + 1) * RNN_HEAD_DIM <= start + GATE_WIN
    return start


GATE_STARTS = tuple(_gate_window_start(j) for j in range(N_GATE_GROUPS))

REST_PIECE = 2 * V7X_MXU_DIM
N_REST_PIECES = (D_IN - OFF_B_X) // REST_PIECE
REST_B_GATE = OFF_B_GATE - OFF_B_X
REST_G_A = OFF_G_A - OFF_B_X
REST_G_B = OFF_G_B - OFF_B_X
REST_END = D_IN - OFF_B_X
assert N_REST_PIECES * REST_PIECE == REST_END

BF16 = jnp.bfloat16
F32 = jnp.float32


def _dot(a, b):
    return jnp.dot(a, b, preferred_element_type=F32)


def _layer_norm(v, g, b):
    mu = jnp.mean(v, axis=-1, keepdims=True)
    c = v - mu
    var = jnp.mean(c * c, axis=-1, keepdims=True)
    return c * lax.rsqrt(var + LN_EPS) * g + b


def _softplus(v):
    return jnp.maximum(v, 0.0) + jnp.log1p(jnp.exp(-jnp.abs(v)))


def _log2(n):
    assert n > 0 and n & (n - 1) == 0, n
    return n.bit_length() - 1


def _row_permutation(m, inner_src, inner_dst):
    i = lax.broadcasted_iota(jnp.int32, (m, m), 0)
    j = lax.broadcasted_iota(jnp.int32, (m, m), 1)
    src = ((i & (inner_dst - 1)) << _log2(inner_src)) + (i >> _log2(inner_dst))
    return jnp.where(j == src, 1.0, 0.0).astype(BF16)


def _mixer_kernel(x_ref, ctxa_ref, ctxb_ref, h0_ref,
                  w_glu_ref, w_rest_ref, w_dwa_ref, b_dwa_ref, lnag_ref, lnab_ref, w_pa_ref,
                  w_dwb_ref, b_dwb_ref, w_gate_ref, b_ra_ref, b_rx_ref, lam_ref,
                  w_pb_ref, w_out_ref, ln1g_ref, ln1b_ref,
                  x1_ref, newa_ref, newb_ref, newh_ref,
                  ua_buf, ub_halo, h_scr, ca_scr, cbb_scr, a_scr, b_scr, hs_scr, rest_scr,
                  *, n_seq, t_chunk, n_chunks, reset_first, batch_major):
    s = pl.program_id(1)
    m = n_seq * t_chunk
    ha = HALO_A * n_seq
    hb = HALO_B * n_seq

    @pl.when(s == 0)
    def _load_state():
        ua_buf[0:ha, :] = ctxa_ref[...].reshape(ha, D_CONV)
        ub_halo[...] = ctxb_ref[...].reshape(hb, D_RNN)
        h_scr[...] = h0_ref[...]

    x = x_ref[...].reshape(m, D_MODEL)
    xb = x.astype(BF16)
    if batch_major:
        xb = _dot(_row_permutation(m, t_chunk, n_seq), xb).astype(BF16)

    def rest_cols(c0, c1):
        parts = []
        while c0 < c1:
            p, off = divmod(c0, REST_PIECE)
            width = min(c1 - c0, REST_PIECE - off)
            parts.append(rest_scr[p, :, off:off + width])
            c0 += width
        return parts[0] if len(parts) == 1 else jnp.concatenate(parts, axis=1)

    def glu_rows(r0, r1):
        z = _dot(xb[r0:r1], w_glu_ref[...])
        ua_buf[ha + r0:ha + r1, :] = z[:, :D_CONV] * jax.nn.sigmoid(z[:, D_CONV:])

    def conv_rows(r0):
        loaded = {}

        def context(row):
            if row not in loaded:
                loaded[row] = ua_buf[row:row + V7X_SUBLANES, :]
            return loaded[row]

        groups = range(r0, r0 + CONV_ROWS, V7X_SUBLANES)
        accs = {g: jnp.broadcast_to(b_dwa_ref[...], (V7X_SUBLANES, D_CONV)) for g in groups}
        for k in range(CONV_K):
            w_k = w_dwa_ref[k]
            for g in groups:
                accs[g] = accs[g] + w_k * context(g + k * n_seq)
        y = _layer_norm(jnp.concatenate([accs[g] for g in groups], axis=0),
                        lnag_ref[...], lnab_ref[...])
        ca_scr[r0:r0 + CONV_ROWS, :] = (y * jax.nn.sigmoid(y)).astype(BF16)

    glu_block = max(m // GLU_BLOCKS, CONV_ROWS)
    for r0 in range(0, m, glu_block):
        glu_rows(r0, r0 + glu_block)
    for p in range(N_REST_PIECES):
        rest_scr[p] = _dot(xb, w_rest_ref[p])
    for r0 in range(0, m, CONV_ROWS):
        conv_rows(r0)

    y_a = _dot(ca_scr[...], w_pa_ref[...])
    mix_in = jax.nn.sigmoid(rest_cols(REST_G_A, REST_G_B)) * y_a

    for c0 in range(0, D_RNN, GATE_COLS):
        c1 = c0 + GATE_COLS
        ub = jnp.concatenate([ub_halo[:, c0:c1], rest_cols(c0, c1)], axis=0)
        cb = jnp.broadcast_to(b_dwb_ref[:, c0:c1], (m, GATE_COLS))
        for k in range(RNN_CONV_K):
            cb = cb + w_dwb_ref[k:k + 1, c0:c1] * ub[k * n_seq:k * n_seq + m]
        b_scr[:, c0:c1] = cb
        cbb_scr[:, c0:c1] = cb.astype(BF16)
        ub_halo[:, c0:c1] = ub[m:m + hb]

    log_a_scale = -RG_C * _softplus(-lam_ref[...])
    if reset_first:
        row = lax.broadcasted_iota(jnp.int32, (m, 1), 0)
        first = jnp.logical_and(row < n_seq, s == 0)
    for j in range(N_GATE_GROUPS):
        c0, c1 = j * GATE_COLS, (j + 1) * GATE_COLS
        gates = _dot(cbb_scr[:, GATE_STARTS[j]:GATE_STARTS[j] + GATE_WIN], w_gate_ref[j])
        r = jax.nn.sigmoid(gates[:, :GATE_COLS] + b_ra_ref[:, c0:c1])
        ig = jax.nn.sigmoid(gates[:, GATE_COLS:] + b_rx_ref[:, c0:c1])
        a = jnp.exp(log_a_scale[:, c0:c1] * r)
        mult = jnp.sqrt(1.0 - a * a)
        if reset_first:
            a = jnp.where(first, 0.0, a)
            mult = jnp.where(first, 1.0, mult)
        a_scr[:, c0:c1] = a
        b_scr[:, c0:c1] = mult * ig * b_scr[:, c0:c1]

    def scan_body(t, h):
        r0 = pl.multiple_of(t * n_seq, n_seq)
        h = a_scr[pl.ds(r0, n_seq), :] * h + b_scr[pl.ds(r0, n_seq), :]
        hs_scr[pl.ds(r0, n_seq), :] = h
        return h

    h_last = lax.fori_loop(0, t_chunk, scan_body, h_scr[...])
    h_scr[...] = h_last

    yb_in = (hs_scr[...] * jax.nn.gelu(rest_cols(REST_B_GATE, REST_G_A))).astype(BF16)
    y_b = _dot(yb_in, w_pb_ref[...])
    mix_in = (mix_in + jax.nn.sigmoid(rest_cols(REST_G_B, REST_END)) * y_b).astype(BF16)
    if batch_major:
        mix_in = _dot(_row_permutation(m, n_seq, t_chunk), mix_in).astype(BF16)

    mix = _dot(mix_in, w_out_ref[...])
    x1 = _layer_norm(DN_ALPHA * x + mix, ln1g_ref[...], ln1b_ref[...])
    x1_ref[...] = x1.reshape(x1_ref.shape)

    @pl.when(s == n_chunks - 1)
    def _emit_state():
        newa_ref[...] = ua_buf[m:m + ha, :].reshape(HALO_A, n_seq, D_CONV)
        newb_ref[...] = ub_halo[...].reshape(HALO_B, n_seq, D_RNN)
        newh_ref[...] = h_last

    if n_chunks > 1:
        @pl.when(s < n_chunks - 1)
        def _carry():
            ua_buf[0:ha, :] = ua_buf[m:m + ha, :]


def _vmem_full():
    return pl.BlockSpec(memory_space=pltpu.VMEM)


def _mixer(x, ctxa_t, ctxb_t, h0, weights, *, n_seq, t_chunk, reset_first, batch_major):
    if batch_major:
        n_total, t_total, _ = x.shape
        x_block = (n_seq, t_chunk, D_MODEL)
        x_index = lambda g, s: (g, s, 0)
    else:
        t_total, n_total, _ = x.shape
        x_block = (t_chunk, n_seq, D_MODEL)
        x_index = lambda g, s: (s, g, 0)
    n_groups = n_total // n_seq
    n_chunks = t_total // t_chunk
    assert n_groups * n_seq == n_total and n_chunks * t_chunk == t_total
    assert n_seq % V7X_SUBLANES == 0 and (n_seq * t_chunk) % CONV_ROWS == 0
    assert n_chunks == 1 or t_chunk >= HALO_A
    assert not batch_major or t_chunk % 16 == 0
    m = n_seq * t_chunk

    kernel = functools.partial(_mixer_kernel, n_seq=n_seq, t_chunk=t_chunk, n_chunks=n_chunks,
                               reset_first=reset_first, batch_major=batch_major)
    state_specs = [
        pl.BlockSpec((HALO_A, n_seq, D_CONV), lambda g, s: (0, g, 0)),
        pl.BlockSpec((HALO_B, n_seq, D_RNN), lambda g, s: (0, g, 0)),
        pl.BlockSpec((n_seq, D_RNN), lambda g, s: (g, 0)),
    ]
    in_specs = [pl.BlockSpec(x_block, x_index)] + state_specs + [_vmem_full() for _ in weights]
    out_specs = [pl.BlockSpec(x_block, x_index)] + state_specs
    out_shape = [
        jax.ShapeDtypeStruct(x.shape, F32),
        jax.ShapeDtypeStruct((HALO_A, n_total, D_CONV), F32),
        jax.ShapeDtypeStruct((HALO_B, n_total, D_RNN), F32),
        jax.ShapeDtypeStruct((n_total, D_RNN), F32),
    ]
    scratch = [
        pltpu.VMEM(((HALO_A + t_chunk) * n_seq, D_CONV), F32),
        pltpu.VMEM((HALO_B * n_seq, D_RNN), F32),
        pltpu.VMEM((n_seq, D_RNN), F32),
        pltpu.VMEM((m, D_CONV), BF16),
        pltpu.VMEM((m, D_RNN), BF16),
        pltpu.VMEM((m, D_RNN), F32),
        pltpu.VMEM((m, D_RNN), F32),
        pltpu.VMEM((m, D_RNN), F32),
        pltpu.VMEM((N_REST_PIECES, m, REST_PIECE), F32),
    ]
    return pl.pallas_call(
        kernel,
        grid=(n_groups, n_chunks),
        in_specs=in_specs,
        out_specs=out_specs,
        out_shape=out_shape,
        scratch_shapes=scratch,
        compiler_params=pltpu.CompilerParams(
            dimension_semantics=("arbitrary", "arbitrary"),
            vmem_limit_bytes=V7X_VMEM_LIMIT_BYTES),
        name="mixer",
    )(x, ctxa_t, ctxb_t, h0, *weights)


def _ffn_kernel(x1_ref, p_ref, w1_ref, w2_ref, wg_ref, wp_ref, ln2g_ref, ln2b_ref, y_ref):
    x1 = x1_ref[...]
    x1b = x1.astype(BF16)
    hid = jnp.maximum(_dot(x1b, w1_ref[...]), 0.0)
    ff = _dot((hid * hid).astype(BF16), w2_ref[...])
    ple = jax.nn.sigmoid(_dot(x1b, wg_ref[...])) * _dot(p_ref[...].astype(BF16), wp_ref[...])
    y_ref[...] = _layer_norm(DN_ALPHA * x1 + ff + ple, ln2g_ref[...], ln2b_ref[...])


def _ffn(x1, p, weights):
    n_rows = x1.shape[0]
    rows = min(FFN_ROWS, n_rows)
    assert n_rows % rows == 0
    in_specs = [
        pl.BlockSpec((rows, D_MODEL), lambda i: (i, 0)),
        pl.BlockSpec((rows, D_PLE), lambda i: (i, 0)),
    ] + [_vmem_full() for _ in weights]
    return pl.pallas_call(
        _ffn_kernel,
        grid=(n_rows // rows,),
        in_specs=in_specs,
        out_specs=pl.BlockSpec((rows, D_MODEL), lambda i: (i, 0)),
        out_shape=jax.ShapeDtypeStruct((n_rows, D_MODEL), F32),
        compiler_params=pltpu.CompilerParams(
            dimension_semantics=("arbitrary",),
            vmem_limit_bytes=V7X_VMEM_LIMIT_BYTES),
        name="ffn",
    )(x1, p, *weights)


def _block_diag(w):
    h, i, j = w.shape
    eye = jnp.eye(h, dtype=w.dtype)
    return (w[:, :, None, :] * eye[:, None, :, None]).reshape(h * i, h * j)


def _gate_windows(w_rg_a, w_rg_x):
    dense_a, dense_x = _block_diag(w_rg_a), _block_diag(w_rg_x)
    groups = []
    for j, start in enumerate(GATE_STARTS):
        rows = slice(start, start + GATE_WIN)
        cols = slice(j * GATE_COLS, (j + 1) * GATE_COLS)
        groups.append(jnp.concatenate([dense_a[rows, cols], dense_x[rows, cols]], axis=1))
    return jnp.stack(groups).astype(BF16)


def _time_major(v):
    return jnp.transpose(v, (1, 0, 2))


def kernel(x_prompt, x_sample, state_conv_a, state_conv_b, state_h, p_prompt, p_sample, w_in, w_dw_a, b_dw_a, ln_a_g, ln_a_b, w_proj_a, w_dw_b, b_dw_b, w_rg_a, b_rg_a, w_rg_x, b_rg_x, rg_lam, w_proj_b, w_out, ln1_g, ln1_b, w_ff1, w_ff2, w_ple_gate, w_ple_proj, ln2_g, ln2_b):
    assert w_in.shape[0] == DEPTH == 1
    row = lambda v: v[0].reshape(1, -1)
    w_rest = w_in[0][:, OFF_B_X:].astype(BF16).reshape(D_MODEL, N_REST_PIECES, REST_PIECE)
    mixer_w = (
        w_in[0][:, :OFF_B_X].astype(BF16),
        jnp.transpose(w_rest, (1, 0, 2)),
        jnp.broadcast_to(w_dw_a[0][:, None, :], (CONV_K, V7X_SUBLANES, D_CONV)),
        row(b_dw_a), row(ln_a_g), row(ln_a_b),
        w_proj_a[0].astype(BF16),
        w_dw_b[0], row(b_dw_b),
        _gate_windows(w_rg_a[0], w_rg_x[0]), row(b_rg_a), row(b_rg_x), row(rg_lam),
        w_proj_b[0].astype(BF16), w_out[0].astype(BF16), row(ln1_g), row(ln1_b),
    )
    ffn_w = (
        w_ff1[0].astype(BF16), w_ff2[0].astype(BF16), w_ple_gate[0].astype(BF16),
        w_ple_proj[0].astype(BF16), row(ln2_g), row(ln2_b),
    )

    bp, tp, _ = x_prompt.shape
    x1_p, na_p, nb_p, nh_p = _mixer(
        x_prompt, jnp.zeros((HALO_A, bp, D_CONV), F32), jnp.zeros((HALO_B, bp, D_RNN), F32),
        jnp.zeros((bp, D_RNN), F32), mixer_w,
        n_seq=bp, t_chunk=32, reset_first=True, batch_major=True)
    y_p = _ffn(x1_p.reshape(bp * tp, D_MODEL), p_prompt[0].reshape(bp * tp, D_PLE), ffn_w)

    bs, ts, _ = x_sample.shape
    x1_s, na_s, nb_s, nh_s = _mixer(
        _time_major(x_sample), _time_major(state_conv_a[0]), _time_major(state_conv_b[0]),
        state_h[0], mixer_w, n_seq=32, t_chunk=ts, reset_first=False, batch_major=False)
    y_s = _ffn(x1_s.reshape(ts * bs, D_MODEL), _time_major(p_sample[0]).reshape(ts * bs, D_PLE),
               ffn_w)

    return (y_p.reshape(bp, tp, D_MODEL), _time_major(y_s.reshape(ts, bs, D_MODEL)),
            _time_major(na_p)[None], _time_major(nb_p)[None], nh_p[None],
            _time_major(na_s)[None], _time_major(nb_s)[None], nh_s[None])
```

```python
import functools

import jax
import jax.numpy as jnp
from jax import lax
from jax.experimental import pallas as pl
from jax.experimental.pallas import tpu as pltpu

D_MODEL = 1024
D_CONV = D_MODEL
CONV_K = 31
D_RNN = 1280
RNN_HEADS = 16
RNN_HEAD_DIM = D_RNN // RNN_HEADS
RNN_CONV_K = 4
RG_C = 8.0
D_FF = 4 * D_MODEL
D_PLE = 256
LN_EPS = 1e-5
DEPTH = 1
DN_ALPHA = (2.0 * DEPTH) ** 0.25

OFF_A_VAL = 0
OFF_A_GATE = OFF_A_VAL + D_CONV
OFF_B_X = OFF_A_GATE + D_CONV
OFF_B_GATE = OFF_B_X + D_RNN
OFF_G_A = OFF_B_GATE + D_RNN
OFF_G_B = OFF_G_A + D_MODEL
D_IN = OFF_G_B + D_MODEL

HALO_A = CONV_K - 1
HALO_B = RNN_CONV_K - 1

V7X_VMEM_LIMIT_BYTES = 56 * 1024 * 1024
V7X_LANES = 128
V7X_SUBLANES = 8
V7X_MXU_DIM = 256
CONV_ROWS = 16
GLU_BLOCKS = 2
FFN_ROWS = 512
FFN_ROW_BLOCKS = 2

GATE_COLS = V7X_MXU_DIM
GATE_WIN = 2 * V7X_MXU_DIM
N_GATE_GROUPS = D_RNN // GATE_COLS


def _gate_window_start(j):
    first_head = (j * GATE_COLS) // RNN_HEAD_DIM
    last_head = (j * GATE_COLS + GATE_COLS - 1) // RNN_HEAD_DIM
    start = min((first_head * RNN_HEAD_DIM) // V7X_LANES * V7X_LANES, D_RNN - GATE_WIN)
    assert start <= first_head * RNN_HEAD_DIM
    assert (last_head + 1) * RNN_HEAD_DIM <= start + GATE_WIN
    return start


GATE_STARTS = tuple(_gate_window_start(j) for j in range(N_GATE_GROUPS))

REST_PIECE = 2 * V7X_MXU_DIM
N_REST_PIECES = (D_IN - OFF_B_X) // REST_PIECE
REST_B_GATE = OFF_B_GATE - OFF_B_X
REST_G_A = OFF_G_A - OFF_B_X
REST_G_B = OFF_G_B - OFF_B_X
REST_END = D_IN - OFF_B_X
assert N_REST_PIECES * REST_PIECE == REST_END

BF16 = jnp.bfloat16
F32 = jnp.float32


def _dot(a, b):
    return jnp.dot(a, b, preferred_element_type=F32)


def _layer_norm(v, g, b):
    mu = jnp.mean(v, axis=-1, keepdims=True)
    c = v - mu
    var = jnp.mean(c * c, axis=-1, keepdims=True)
    return c * lax.rsqrt(var + LN_EPS) * g + b


def _softplus(v):
    return jnp.maximum(v, 0.0) + jnp.log1p(jnp.exp(-jnp.abs(v)))


def _log2(n):
    assert n > 0 and n & (n - 1) == 0, n
    return n.bit_length() - 1


def _row_permutation(m, inner_src, inner_dst):
    i = lax.broadcasted_iota(jnp.int32, (m, m), 0)
    j = lax.broadcasted_iota(jnp.int32, (m, m), 1)
    src = ((i & (inner_dst - 1)) << _log2(inner_src)) + (i >> _log2(inner_dst))
    return jnp.where(j == src, 1.0, 0.0).astype(BF16)


def _mixer_kernel(x_ref, ctxa_ref, ctxb_ref, h0_ref,
                  w_in_ref, w_dwa_ref, b_dwa_ref, lnag_ref, lnab_ref, w_pa_ref,
                  w_dwb_ref, b_dwb_ref, w_gate_ref, b_ra_ref, b_rx_ref, lam_ref,
                  w_pb_ref, w_out_ref, ln1g_ref, ln1b_ref,
                  x1_ref, newa_ref, newb_ref, newh_ref,
                  ua_buf, ub_halo, h_scr, ca_scr, cbb_scr, a_scr, b_scr, hs_scr, rest_scr,
                  *, n_seq, t_chunk, n_chunks, reset_first, batch_major):
    s = pl.program_id(1)
    m = n_seq * t_chunk
    ha = HALO_A * n_seq
    hb = HALO_B * n_seq

    @pl.when(s == 0)
    def _load_state():
        ua_buf[0:ha, :] = ctxa_ref[...].reshape(ha, D_CONV)
        ub_halo[...] = ctxb_ref[...].reshape(hb, D_RNN)
        h_scr[...] = h0_ref[...]

    x = x_ref[...].reshape(m, D_MODEL)
    xb = x.astype(BF16)
    if batch_major:
        xb = _dot(_row_permutation(m, t_chunk, n_seq), xb).astype(BF16)

    def rest_cols(c0, c1):
        parts = []
        while c0 < c1:
            p, off = divmod(c0, REST_PIECE)
            width = min(c1 - c0, REST_PIECE - off)
            parts.append(rest_scr[p, :, off:off + width])
            c0 += width
        return parts[0] if len(parts) == 1 else jnp.concatenate(parts, axis=1)

    def glu_rows(r0, r1):
        z = _dot(xb[r0:r1], w_in_ref[:, OFF_A_VAL:OFF_B_X])
        ua_buf[ha + r0:ha + r1, :] = z[:, :D_CONV] * jax.nn.sigmoid(z[:, D_CONV:])

    def conv_rows(r0):
        loaded = {}

        def context(row):
            if row not in loaded:
                loaded[row] = ua_buf[row:row + V7X_SUBLANES, :]
            return loaded[row]

        groups = range(r0, r0 + CONV_ROWS, V7X_SUBLANES)
        accs = {g: jnp.broadcast_to(b_dwa_ref[...], (V7X_SUBLANES, D_CONV)) for g in groups}
        for k in range(CONV_K):
            w_k = w_dwa_ref[k]
            for g in groups:
                accs[g] = accs[g] + w_k * context(g + k * n_seq)
        y = _layer_norm(jnp.concatenate([accs[g] for g in groups], axis=0),
                        lnag_ref[...], lnab_ref[...])
        ca_scr[r0:r0 + CONV_ROWS, :] = (y * jax.nn.sigmoid(y)).astype(BF16)

    glu_block = max(m // GLU_BLOCKS, CONV_ROWS)
    for r0 in range(0, m, glu_block):
        glu_rows(r0, r0 + glu_block)
    for p in range(N_REST_PIECES):
        c0 = OFF_B_X + p * REST_PIECE
        rest_scr[p] = _dot(xb, w_in_ref[:, c0:c0 + REST_PIECE])
    for r0 in range(0, m, CONV_ROWS):
        conv_rows(r0)

    y_a = _dot(ca_scr[...], w_pa_ref[...])
    mix_in = jax.nn.sigmoid(rest_cols(REST_G_A, REST_G_B)) * y_a

    for c0 in range(0, D_RNN, GATE_COLS):
        c1 = c0 + GATE_COLS
        ub = jnp.concatenate([ub_halo[:, c0:c1], rest_cols(c0, c1)], axis=0)
        cb = jnp.broadcast_to(b_dwb_ref[:, c0:c1], (m, GATE_COLS))
        for k in range(RNN_CONV_K):
            cb = cb + w_dwb_ref[k:k + 1, c0:c1] * ub[k * n_seq:k * n_seq + m]
        b_scr[:, c0:c1] = cb
        cbb_scr[:, c0:c1] = cb.astype(BF16)
        ub_halo[:, c0:c1] = ub[m:m + hb]

    log_a_scale = -RG_C * _softplus(-lam_ref[...])
    if reset_first:
        row = lax.broadcasted_iota(jnp.int32, (m, 1), 0)
        first = jnp.logical_and(row < n_seq, s == 0)
    for j in range(N_GATE_GROUPS):
        c0, c1 = j * GATE_COLS, (j + 1) * GATE_COLS
        gates = _dot(cbb_scr[:, GATE_STARTS[j]:GATE_STARTS[j] + GATE_WIN], w_gate_ref[j])
        r = jax.nn.sigmoid(gates[:, :GATE_COLS] + b_ra_ref[:, c0:c1])
        ig = jax.nn.sigmoid(gates[:, GATE_COLS:] + b_rx_ref[:, c0:c1])
        a = jnp.exp(log_a_scale[:, c0:c1] * r)
        mult = jnp.sqrt(1.0 - a * a)
        if reset_first:
            a = jnp.where(first, 0.0, a)
            mult = jnp.where(first, 1.0, mult)
        a_scr[:, c0:c1] = a
        b_scr[:, c0:c1] = mult * ig * b_scr[:, c0:c1]

    def scan_body(t, h):
        r0 = pl.multiple_of(t * n_seq, n_seq)
        h = a_scr[pl.ds(r0, n_seq), :] * h + b_scr[pl.ds(r0, n_seq), :]
        hs_scr[pl.ds(r0, n_seq), :] = h
        return h

    h_last = lax.fori_loop(0, t_chunk, scan_body, h_scr[...])
    h_scr[...] = h_last

    yb_in = (hs_scr[...] * jax.nn.gelu(rest_cols(REST_B_GATE, REST_G_A))).astype(BF16)
    y_b = _dot(yb_in, w_pb_ref[...])
    mix_in = (mix_in + jax.nn.sigmoid(rest_cols(REST_G_B, REST_END)) * y_b).astype(BF16)
    if batch_major:
        mix_in = _dot(_row_permutation(m, n_seq, t_chunk), mix_in).astype(BF16)

    mix = _dot(mix_in, w_out_ref[...])
    x1 = _layer_norm(DN_ALPHA * x + mix, ln1g_ref[...], ln1b_ref[...])
    x1_ref[...] = x1.reshape(x1_ref.shape)

    @pl.when(s == n_chunks - 1)
    def _emit_state():
        newa_ref[...] = ua_buf[m:m + ha, :].reshape(HALO_A, n_seq, D_CONV)
        newb_ref[...] = ub_halo[...].reshape(HALO_B, n_seq, D_RNN)
        newh_ref[...] = h_last

    if n_chunks > 1:
        @pl.when(s < n_chunks - 1)
        def _carry():
            ua_buf[0:ha, :] = ua_buf[m:m + ha, :]


def _vmem_full():
    return pl.BlockSpec(memory_space=pltpu.VMEM)


def _mixer(x, ctxa_t, ctxb_t, h0, weights, *, n_seq, t_chunk, reset_first, batch_major):
    if batch_major:
        n_total, t_total, _ = x.shape
        x_block = (n_seq, t_chunk, D_MODEL)
        x_index = lambda g, s: (g, s, 0)
    else:
        t_total, n_total, _ = x.shape
        x_block = (t_chunk, n_seq, D_MODEL)
        x_index = lambda g, s: (s, g, 0)
    n_groups = n_total // n_seq
    n_chunks = t_total // t_chunk
    assert n_groups * n_seq == n_total and n_chunks * t_chunk == t_total
    assert n_seq % V7X_SUBLANES == 0 and (n_seq * t_chunk) % CONV_ROWS == 0
    assert n_chunks == 1 or t_chunk >= HALO_A
    assert not batch_major or t_chunk % 16 == 0
    m = n_seq * t_chunk

    kernel = functools.partial(_mixer_kernel, n_seq=n_seq, t_chunk=t_chunk, n_chunks=n_chunks,
                               reset_first=reset_first, batch_major=batch_major)
    state_specs = [
        pl.BlockSpec((HALO_A, n_seq, D_CONV), lambda g, s: (0, g, 0)),
        pl.BlockSpec((HALO_B, n_seq, D_RNN), lambda g, s: (0, g, 0)),
        pl.BlockSpec((n_seq, D_RNN), lambda g, s: (g, 0)),
    ]
    in_specs = [pl.BlockSpec(x_block, x_index)] + state_specs + [_vmem_full() for _ in weights]
    out_specs = [pl.BlockSpec(x_block, x_index)] + state_specs
    out_shape = [
        jax.ShapeDtypeStruct(x.shape, F32),
        jax.ShapeDtypeStruct((HALO_A, n_total, D_CONV), F32),
        jax.ShapeDtypeStruct((HALO_B, n_total, D_RNN), F32),
        jax.ShapeDtypeStruct((n_total, D_RNN), F32),
    ]
    scratch = [
        pltpu.VMEM(((HALO_A + t_chunk) * n_seq, D_CONV), F32),
        pltpu.VMEM((HALO_B * n_seq, D_RNN), F32),
        pltpu.VMEM((n_seq, D_RNN), F32),
        pltpu.VMEM((m, D_CONV), BF16),
        pltpu.VMEM((m, D_RNN), BF16),
        pltpu.VMEM((m, D_RNN), F32),
        pltpu.VMEM((m, D_RNN), F32),
        pltpu.VMEM((m, D_RNN), F32),
        pltpu.VMEM((N_REST_PIECES, m, REST_PIECE), F32),
    ]
    return pl.pallas_call(
        kernel,
        grid=(n_groups, n_chunks),
        in_specs=in_specs,
        out_specs=out_specs,
        out_shape=out_shape,
        scratch_shapes=scratch,
        compiler_params=pltpu.CompilerParams(
            dimension_semantics=("arbitrary", "arbitrary"),
            vmem_limit_bytes=V7X_VMEM_LIMIT_BYTES),
        name="mixer",
    )(x, ctxa_t, ctxb_t, h0, *weights)


def _ffn_kernel(x1_ref, p_ref, w1_ref, w2_ref, wg_ref, wp_ref, ln2g_ref, ln2b_ref, y_ref):
    n_rows = x1_ref.shape[0]
    block = max(n_rows // FFN_ROW_BLOCKS, 16)
    for r0 in range(0, n_rows, block):
        x1 = x1_ref[r0:r0 + block, :]
        x1b = x1.astype(BF16)
        hid = jnp.maximum(_dot(x1b, w1_ref[...]), 0.0)
        ff = _dot((hid * hid).astype(BF16), w2_ref[...])
        ple = (jax.nn.sigmoid(_dot(x1b, wg_ref[...]))
               * _dot(p_ref[r0:r0 + block, :].astype(BF16), wp_ref[...]))
        y_ref[r0:r0 + block, :] = _layer_norm(DN_ALPHA * x1 + ff + ple,
                                              ln2g_ref[...], ln2b_ref[...])


def _ffn(x1, p, weights):
    n_rows = x1.shape[0]
    rows = min(FFN_ROWS, n_rows)
    assert n_rows % rows == 0
    in_specs = [
        pl.BlockSpec((rows, D_MODEL), lambda i: (i, 0)),
        pl.BlockSpec((rows, D_PLE), lambda i: (i, 0)),
    ] + [_vmem_full() for _ in weights]
    return pl.pallas_call(
        _ffn_kernel,
        grid=(n_rows // rows,),
        in_specs=in_specs,
        out_specs=pl.BlockSpec((rows, D_MODEL), lambda i: (i, 0)),
        out_shape=jax.ShapeDtypeStruct((n_rows, D_MODEL), F32),
        compiler_params=pltpu.CompilerParams(
            dimension_semantics=("arbitrary",),
            vmem_limit_bytes=V7X_VMEM_LIMIT_BYTES),
        name="ffn",
    )(x1, p, *weights)


def _block_diag(w):
    h, i, j = w.shape
    eye = jnp.eye(h, dtype=w.dtype)
    return (w[:, :, None, :] * eye[:, None, :, None]).reshape(h * i, h * j)


def _gate_windows(w_rg_a, w_rg_x):
    dense_a, dense_x = _block_diag(w_rg_a.astype(BF16)), _block_diag(w_rg_x.astype(BF16))
    groups = []
    for j, start in enumerate(GATE_STARTS):
        rows = slice(start, start + GATE_WIN)
        cols = slice(j * GATE_COLS, (j + 1) * GATE_COLS)
        groups.append(jnp.concatenate([dense_a[rows, cols], dense_x[rows, cols]], axis=1))
    return jnp.stack(groups)


def _time_major(v):
    return jnp.transpose(v, (1, 0, 2))


def kernel(x_prompt, x_sample, state_conv_a, state_conv_b, state_h, p_prompt, p_sample, w_in, w_dw_a, b_dw_a, ln_a_g, ln_a_b, w_proj_a, w_dw_b, b_dw_b, w_rg_a, b_rg_a, w_rg_x, b_rg_x, rg_lam, w_proj_b, w_out, ln1_g, ln1_b, w_ff1, w_ff2, w_ple_gate, w_ple_proj, ln2_g, ln2_b):
    assert w_in.shape[0] == DEPTH == 1
    row = lambda v: v[0].reshape(1, -1)
    mixer_w = (
        w_in[0].astype(BF16),
        jnp.broadcast_to(w_dw_a[0][:, None, :], (CONV_K, V7X_SUBLANES, D_CONV)),
        row(b_dw_a), row(ln_a_g), row(ln_a_b),
        w_proj_a[0].astype(BF16),
        w_dw_b[0], row(b_dw_b),
        _gate_windows(w_rg_a[0], w_rg_x[0]), row(b_rg_a), row(b_rg_x), row(rg_lam),
        w_proj_b[0].astype(BF16), w_out[0].astype(BF16), row(ln1_g), row(ln1_b),
    )
    ffn_w = (
        w_ff1[0].astype(BF16), w_ff2[0].astype(BF16), w_ple_gate[0].astype(BF16),
        w_ple_proj[0].astype(BF16), row(ln2_g), row(ln2_b),
    )

    bp, tp, _ = x_prompt.shape
    x1_p, na_p, nb_p, nh_p = _mixer(
        x_prompt, jnp.zeros((HALO_A, bp, D_CONV), F32), jnp.zeros((HALO_B, bp, D_RNN), F32),
        jnp.zeros((bp, D_RNN), F32), mixer_w,
        n_seq=bp, t_chunk=32, reset_first=True, batch_major=True)
    y_p = _ffn(x1_p.reshape(bp * tp, D_MODEL), p_prompt[0].reshape(bp * tp, D_PLE), ffn_w)

    bs, ts, _ = x_sample.shape
    x1_s, na_s, nb_s, nh_s = _mixer(
        _time_major(x_sample), _time_major(state_conv_a[0]), _time_major(state_conv_b[0]),
        state_h[0], mixer_w, n_seq=32, t_chunk=ts, reset_first=False, batch_major=False)
    y_s = _ffn(x1_s.reshape(ts * bs, D_MODEL), _time_major(p_sample[0]).reshape(ts * bs, D_PLE),
               ffn_w)

    return (y_p.reshape(bp, tp, D_MODEL), _time_major(y_s.reshape(ts, bs, D_MODEL)),
            _time_major(na_p)[None], _time_major(nb_p)[None], nh_p[None],
            _time_major(na_s)[None], _time_major(nb_s)[None], nh_s[None])
```

```python
import functools

import jax
import jax.numpy as jnp
from jax import lax
from jax.experimental import pallas as pl
from jax.experimental.pallas import tpu as pltpu

D_MODEL = 1024
D_CONV = D_MODEL
CONV_K = 31
D_RNN = 1280
RNN_HEADS = 16
RNN_HEAD_DIM = D_RNN // RNN_HEADS
RNN_CONV_K = 4
RG_C = 8.0
D_FF = 4 * D_MODEL
D_PLE = 256
LN_EPS = 1e-5
DEPTH = 1
DN_ALPHA = (2.0 * DEPTH) ** 0.25

OFF_A_VAL = 0
OFF_A_GATE = OFF_A_VAL + D_CONV
OFF_B_X = OFF_A_GATE + D_CONV
OFF_B_GATE = OFF_B_X + D_RNN
OFF_G_A = OFF_B_GATE + D_RNN
OFF_G_B = OFF_G_A + D_MODEL
D_IN = OFF_G_B + D_MODEL

HALO_A = CONV_K - 1
HALO_B = RNN_CONV_K - 1

V7X_VMEM_LIMIT_BYTES = 56 * 1024 * 1024
V7X_LANES = 128
V7X_SUBLANES = 8
V7X_MXU_DIM = 256
CONV_ROWS = 16
GLU_BLOCKS = 2
FFN_ROWS = 512
FFN_ROW_BLOCKS = 2

GATE_COLS = V7X_MXU_DIM
GATE_WIN = 2 * V7X_MXU_DIM
N_GATE_GROUPS = D_RNN // GATE_COLS


def _gate_window_start(j):
    first_head = (j * GATE_COLS) // RNN_HEAD_DIM
    last_head = (j * GATE_COLS + GATE_COLS - 1) // RNN_HEAD_DIM
    start = min((first_head * RNN_HEAD_DIM) // V7X_LANES * V7X_LANES, D_RNN - GATE_WIN)
    assert start <= first_head * RNN_HEAD_DIM
    assert (last_head + 1) * RNN_HEAD_DIM <= start + GATE_WIN
    return start


GATE_STARTS = tuple(_gate_window_start(j) for j in range(N_GATE_GROUPS))

REST_PIECE = 2 * V7X_MXU_DIM
N_REST_PIECES = (D_IN - OFF_B_X) // REST_PIECE
REST_B_GATE = OFF_B_GATE - OFF_B_X
REST_G_A = OFF_G_A - OFF_B_X
REST_G_B = OFF_G_B - OFF_B_X
REST_END = D_IN - OFF_B_X
assert N_REST_PIECES * REST_PIECE == REST_END

BF16 = jnp.bfloat16
F32 = jnp.float32


def _dot(a, b):
    return jnp.dot(a, b, preferred_element_type=F32)


def _layer_norm(v, g, b):
    mu = jnp.mean(v, axis=-1, keepdims=True)
    c = v - mu
    var = jnp.mean(c * c, axis=-1, keepdims=True)
    return c * lax.rsqrt(var + LN_EPS) * g + b


def _softplus(v):
    return jnp.maximum(v, 0.0) + jnp.log1p(jnp.exp(-jnp.abs(v)))


def _log2(n):
    assert n > 0 and n & (n - 1) == 0, n
    return n.bit_length() - 1


def _row_permutation(m, inner_src, inner_dst):
    i = lax.broadcasted_iota(jnp.int32, (m, m), 0)
    j = lax.broadcasted_iota(jnp.int32, (m, m), 1)
    src = ((i & (inner_dst - 1)) << _log2(inner_src)) + (i >> _log2(inner_dst))
    return jnp.where(j == src, 1.0, 0.0).astype(BF16)


def _mixer_kernel(x_ref, ctxa_ref, ctxb_ref, h0_ref,
                  w_in_ref, w_dwa_ref, b_dwa_ref, lnag_ref, lnab_ref, w_pa_ref,
                  w_dwb_ref, b_dwb_ref, w_gate_ref, b_ra_ref, b_rx_ref, lam_ref,
                  w_pb_ref, w_out_ref, ln1g_ref, ln1b_ref,
                  x1_ref, newa_ref, newb_ref, newh_ref,
                  ua_buf, ub_halo, h_scr, ca_scr, cbb_scr, a_scr, b_scr, hs_scr, rest_scr,
                  *, n_seq, t_chunk, n_chunks, reset_first, batch_major):
    s = pl.program_id(1)
    m = n_seq * t_chunk
    ha = HALO_A * n_seq
    hb = HALO_B * n_seq

    @pl.when(s == 0)
    def _load_state():
        ua_buf[0:ha, :] = ctxa_ref[...].reshape(ha, D_CONV)
        ub_halo[...] = ctxb_ref[...].reshape(hb, D_RNN)
        h_scr[...] = h0_ref[...]

    x = x_ref[...].reshape(m, D_MODEL)
    xb = x.astype(BF16)
    if batch_major:
        xb = _dot(_row_permutation(m, t_chunk, n_seq), xb).astype(BF16)

    def rest_cols(c0, c1):
        parts = []
        while c0 < c1:
            p, off = divmod(c0, REST_PIECE)
            width = min(c1 - c0, REST_PIECE - off)
            parts.append(rest_scr[p, :, off:off + width])
            c0 += width
        return parts[0] if len(parts) == 1 else jnp.concatenate(parts, axis=1)

    def glu_rows(r0, r1):
        z = _dot(xb[r0:r1], w_in_ref[:, OFF_A_VAL:OFF_B_X])
        ua_buf[ha + r0:ha + r1, :] = z[:, :D_CONV] * jax.nn.sigmoid(z[:, D_CONV:])

    def conv_rows(r0):
        loaded = {}

        def context(row):
            if row not in loaded:
                loaded[row] = ua_buf[row:row + V7X_SUBLANES, :]
            return loaded[row]

        groups = range(r0, r0 + CONV_ROWS, V7X_SUBLANES)
        accs = {g: jnp.broadcast_to(b_dwa_ref[...], (V7X_SUBLANES, D_CONV)) for g in groups}
        for k in range(CONV_K):
            w_k = w_dwa_ref[k]
            for g in groups:
                accs[g] = accs[g] + w_k * context(g + k * n_seq)
        y = _layer_norm(jnp.concatenate([accs[g] for g in groups], axis=0),
                        lnag_ref[...], lnab_ref[...])
        ca_scr[r0:r0 + CONV_ROWS, :] = (y * jax.nn.sigmoid(y)).astype(BF16)

    glu_block = max(m // GLU_BLOCKS, CONV_ROWS)
    for r0 in range(0, m, glu_block):
        glu_rows(r0, r0 + glu_block)
    for p in range(N_REST_PIECES):
        c0 = OFF_B_X + p * REST_PIECE
        rest_scr[p] = _dot(xb, w_in_ref[:, c0:c0 + REST_PIECE])
    for r0 in range(0, m, CONV_ROWS):
        conv_rows(r0)

    y_a = _dot(ca_scr[...], w_pa_ref[...])
    mix_in = jax.nn.sigmoid(rest_cols(REST_G_A, REST_G_B)) * y_a

    for c0 in range(0, D_RNN, GATE_COLS):
        c1 = c0 + GATE_COLS
        ub = jnp.concatenate([ub_halo[:, c0:c1], rest_cols(c0, c1)], axis=0)
        cb = jnp.broadcast_to(b_dwb_ref[:, c0:c1], (m, GATE_COLS))
        for k in range(RNN_CONV_K):
            cb = cb + w_dwb_ref[k:k + 1, c0:c1] * ub[k * n_seq:k * n_seq + m]
        b_scr[:, c0:c1] = cb
        cbb_scr[:, c0:c1] = cb.astype(BF16)
        ub_halo[:, c0:c1] = ub[m:m + hb]

    log_a_scale = -RG_C * _softplus(-lam_ref[...])
    if reset_first:
        row = lax.broadcasted_iota(jnp.int32, (m, 1), 0)
        first = jnp.logical_and(row < n_seq, s == 0)
    for j in range(N_GATE_GROUPS):
        c0, c1 = j * GATE_COLS, (j + 1) * GATE_COLS
        gates = _dot(cbb_scr[:, GATE_STARTS[j]:GATE_STARTS[j] + GATE_WIN], w_gate_ref[j])
        r = jax.nn.sigmoid(gates[:, :GATE_COLS] + b_ra_ref[:, c0:c1])
        ig = jax.nn.sigmoid(gates[:, GATE_COLS:] + b_rx_ref[:, c0:c1])
        a = jnp.exp(log_a_scale[:, c0:c1] * r)
        mult = jnp.sqrt(1.0 - a * a)
        if reset_first:
            a = jnp.where(first, 0.0, a)
            mult = jnp.where(first, 1.0, mult)
        a_scr[:, c0:c1] = a
        b_scr[:, c0:c1] = mult * ig * b_scr[:, c0:c1]

    def scan_body(t, h):
        r0 = pl.multiple_of(t * n_seq, n_seq)
        h = a_scr[pl.ds(r0, n_seq), :] * h + b_scr[pl.ds(r0, n_seq), :]
        hs_scr[pl.ds(r0, n_seq), :] = h
        return h

    h_last = lax.fori_loop(0, t_chunk, scan_body, h_scr[...], unroll=True)
    h_scr[...] = h_last

    yb_in = (hs_scr[...] * jax.nn.gelu(rest_cols(REST_B_GATE, REST_G_A))).astype(BF16)
    y_b = _dot(yb_in, w_pb_ref[...])
    mix_in = (mix_in + jax.nn.sigmoid(rest_cols(REST_G_B, REST_END)) * y_b).astype(BF16)
    if batch_major:
        mix_in = _dot(_row_permutation(m, n_seq, t_chunk), mix_in).astype(BF16)

    mix = _dot(mix_in, w_out_ref[...])
    x1 = _layer_norm(DN_ALPHA * x + mix, ln1g_ref[...], ln1b_ref[...])
    x1_ref[...] = x1.reshape(x1_ref.shape)

    @pl.when(s == n_chunks - 1)
    def _emit_state():
        newa_ref[...] = ua_buf[m:m + ha, :].reshape(HALO_A, n_seq, D_CONV)
        newb_ref[...] = ub_halo[...].reshape(HALO_B, n_seq, D_RNN)
        newh_ref[...] = h_last

    if n_chunks > 1:
        @pl.when(s < n_chunks - 1)
        def _carry():
            ua_buf[0:ha, :] = ua_buf[m:m + ha, :]


def _vmem_full():
    return pl.BlockSpec(memory_space=pltpu.VMEM)


def _mixer(x, ctxa_t, ctxb_t, h0, weights, *, n_seq, t_chunk, reset_first, batch_major):
    if batch_major:
        n_total, t_total, _ = x.shape
        x_block = (n_seq, t_chunk, D_MODEL)
        x_index = lambda g, s: (g, s, 0)
    else:
        t_total, n_total, _ = x.shape
        x_block = (t_chunk, n_seq, D_MODEL)
        x_index = lambda g, s: (s, g, 0)
    n_groups = n_total // n_seq
    n_chunks = t_total // t_chunk
    assert n_groups * n_seq == n_total and n_chunks * t_chunk == t_total
    assert n_seq % V7X_SUBLANES == 0 and (n_seq * t_chunk) % CONV_ROWS == 0
    assert n_chunks == 1 or t_chunk >= HALO_A
    assert not batch_major or t_chunk % 16 == 0
    m = n_seq * t_chunk

    kernel = functools.partial(_mixer_kernel, n_seq=n_seq, t_chunk=t_chunk, n_chunks=n_chunks,
                               reset_first=reset_first, batch_major=batch_major)
    state_specs = [
        pl.BlockSpec((HALO_A, n_seq, D_CONV), lambda g, s: (0, g, 0)),
        pl.BlockSpec((HALO_B, n_seq, D_RNN), lambda g, s: (0, g, 0)),
        pl.BlockSpec((n_seq, D_RNN), lambda g, s: (g, 0)),
    ]
    in_specs = [pl.BlockSpec(x_block, x_index)] + state_specs + [_vmem_full() for _ in weights]
    out_specs = [pl.BlockSpec(x_block, x_index)] + state_specs
    out_shape = [
        jax.ShapeDtypeStruct(x.shape, F32),
        jax.ShapeDtypeStruct((HALO_A, n_total, D_CONV), F32),
        jax.ShapeDtypeStruct((HALO_B, n_total, D_RNN), F32),
        jax.ShapeDtypeStruct((n_total, D_RNN), F32),
    ]
    scratch = [
        pltpu.VMEM(((HALO_A + t_chunk) * n_seq, D_CONV), F32),
        pltpu.VMEM((HALO_B * n_seq, D_RNN), F32),
        pltpu.VMEM((n_seq, D_RNN), F32),
        pltpu.VMEM((m, D_CONV), BF16),
        pltpu.VMEM((m, D_RNN), BF16),
        pltpu.VMEM((m, D_RNN), F32),
        pltpu.VMEM((m, D_RNN), F32),
        pltpu.VMEM((m, D_RNN), F32),
        pltpu.VMEM((N_REST_PIECES, m, REST_PIECE), F32),
    ]
    return pl.pallas_call(
        kernel,
        grid=(n_groups, n_chunks),
        in_specs=in_specs,
        out_specs=out_specs,
        out_shape=out_shape,
        scratch_shapes=scratch,
        compiler_params=pltpu.CompilerParams(
            dimension_semantics=("arbitrary", "arbitrary"),
            vmem_limit_bytes=V7X_VMEM_LIMIT_BYTES),
        name="mixer",
    )(x, ctxa_t, ctxb_t, h0, *weights)


def _ffn_kernel(x1_ref, p_ref, w1_ref, w2_ref, wg_ref, wp_ref, ln2g_ref, ln2b_ref, y_ref):
    n_rows = x1_ref.shape[0]
    block = max(n_rows // FFN_ROW_BLOCKS, 16)
    for r0 in range(0, n_rows, block):
        x1 = x1_ref[r0:r0 + block, :]
        x1b = x1.astype(BF16)
        hid = jnp.maximum(_dot(x1b, w1_ref[...]), 0.0)
        ff = _dot((hid * hid).astype(BF16), w2_ref[...])
        ple = (jax.nn.sigmoid(_dot(x1b, wg_ref[...]))
               * _dot(p_ref[r0:r0 + block, :].astype(BF16), wp_ref[...]))
        y_ref[r0:r0 + block, :] = _layer_norm(DN_ALPHA * x1 + ff + ple,
                                              ln2g_ref[...], ln2b_ref[...])


def _ffn(x1, p, weights):
    n_rows = x1.shape[0]
    rows = min(FFN_ROWS, n_rows)
    assert n_rows % rows == 0
    in_specs = [
        pl.BlockSpec((rows, D_MODEL), lambda i: (i, 0)),
        pl.BlockSpec((rows, D_PLE), lambda i: (i, 0)),
    ] + [_vmem_full() for _ in weights]
    return pl.pallas_call(
        _ffn_kernel,
        grid=(n_rows // rows,),
        in_specs=in_specs,
        out_specs=pl.BlockSpec((rows, D_MODEL), lambda i: (i, 0)),
        out_shape=jax.ShapeDtypeStruct((n_rows, D_MODEL), F32),
        compiler_params=pltpu.CompilerParams(
            dimension_semantics=("arbitrary",),
            vmem_limit_bytes=V7X_VMEM_LIMIT_BYTES),
        name="ffn",
    )(x1, p, *weights)


def _block_diag(w):
    h, i, j = w.shape
    eye = jnp.eye(h, dtype=w.dtype)
    return (w[:, :, None, :] * eye[:, None, :, None]).reshape(h * i, h * j)


def _gate_windows(w_rg_a, w_rg_x):
    dense_a, dense_x = _block_diag(w_rg_a.astype(BF16)), _block_diag(w_rg_x.astype(BF16))
    groups = []
    for j, start in enumerate(GATE_STARTS):
        rows = slice(start, start + GATE_WIN)
        cols = slice(j * GATE_COLS, (j + 1) * GATE_COLS)
        groups.append(jnp.concatenate([dense_a[rows, cols], dense_x[rows, cols]], axis=1))
    return jnp.stack(groups)


def _time_major(v):
    return jnp.transpose(v, (1, 0, 2))


def kernel(x_prompt, x_sample, state_conv_a, state_conv_b, state_h, p_prompt, p_sample, w_in, w_dw_a, b_dw_a, ln_a_g, ln_a_b, w_proj_a, w_dw_b, b_dw_b, w_rg_a, b_rg_a, w_rg_x, b_rg_x, rg_lam, w_proj_b, w_out, ln1_g, ln1_b, w_ff1, w_ff2, w_ple_gate, w_ple_proj, ln2_g, ln2_b):
    assert w_in.shape[0] == DEPTH == 1
    row = lambda v: v[0].reshape(1, -1)
    mixer_w = (
        w_in[0].astype(BF16),
        jnp.broadcast_to(w_dw_a[0][:, None, :], (CONV_K, V7X_SUBLANES, D_CONV)),
        row(b_dw_a), row(ln_a_g), row(ln_a_b),
        w_proj_a[0].astype(BF16),
        w_dw_b[0], row(b_dw_b),
        _gate_windows(w_rg_a[0], w_rg_x[0]), row(b_rg_a), row(b_rg_x), row(rg_lam),
        w_proj_b[0].astype(BF16), w_out[0].astype(BF16), row(ln1_g), row(ln1_b),
    )
    ffn_w = (
        w_ff1[0].astype(BF16), w_ff2[0].astype(BF16), w_ple_gate[0].astype(BF16),
        w_ple_proj[0].astype(BF16), row(ln2_g), row(ln2_b),
    )

    bp, tp, _ = x_prompt.shape
    x1_p, na_p, nb_p, nh_p = _mixer(
        x_prompt, jnp.zeros((HALO_A, bp, D_CONV), F32), jnp.zeros((HALO_B, bp, D_RNN), F32),
        jnp.zeros((bp, D_RNN), F32), mixer_w,
        n_seq=bp, t_chunk=32, reset_first=True, batch_major=True)
    y_p = _ffn(x1_p.reshape(bp * tp, D_MODEL), p_prompt[0].reshape(bp * tp, D_PLE), ffn_w)

    bs, ts, _ = x_sample.shape
    x1_s, na_s, nb_s, nh_s = _mixer(
        _time_major(x_sample), _time_major(state_conv_a[0]), _time_major(state_conv_b[0]),
        state_h[0], mixer_w, n_seq=32, t_chunk=ts, reset_first=False, batch_major=False)
    y_s = _ffn(x1_s.reshape(ts * bs, D_MODEL), _time_major(p_sample[0]).reshape(ts * bs, D_PLE),
               ffn_w)

    return (y_p.reshape(bp, tp, D_MODEL), _time_major(y_s.reshape(ts, bs, D_MODEL)),
            _time_major(na_p)[None], _time_major(nb_p)[None], nh_p[None],
            _time_major(na_s)[None], _time_major(nb_s)[None], nh_s[None])
```

```python
import functools

import jax
import jax.numpy as jnp
from jax import lax
from jax.experimental import pallas as pl
from jax.experimental.pallas import tpu as pltpu

D_MODEL = 1024
D_CONV = D_MODEL
CONV_K = 31
D_RNN = 1280
RNN_HEADS = 16
RNN_HEAD_DIM = D_RNN // RNN_HEADS
RNN_CONV_K = 4
RG_C = 8.0
D_FF = 4 * D_MODEL
D_PLE = 256
LN_EPS = 1e-5
DEPTH = 1
DN_ALPHA = (2.0 * DEPTH) ** 0.25

OFF_A_VAL = 0
OFF_A_GATE = OFF_A_VAL + D_CONV
OFF_B_X = OFF_A_GATE + D_CONV
OFF_B_GATE = OFF_B_X + D_RNN
OFF_G_A = OFF_B_GATE + D_RNN
OFF_G_B = OFF_G_A + D_MODEL
D_IN = OFF_G_B + D_MODEL

HALO_A = CONV_K - 1
HALO_B = RNN_CONV_K - 1

V7X_VMEM_LIMIT_BYTES = 56 * 1024 * 1024
V7X_LANES = 128
V7X_SUBLANES = 8
V7X_MXU_DIM = 256
CONV_ROWS = 16
CONV_TAP_ROT = 2 * V7X_LANES
GLU_BLOCKS = 2
FFN_ROWS = 1024
FFN_ROW_BLOCKS = 4

GATE_COLS = V7X_MXU_DIM
GATE_WIN = 2 * V7X_MXU_DIM
N_GATE_GROUPS = D_RNN // GATE_COLS


def _gate_window_start(j):
    first_head = (j * GATE_COLS) // RNN_HEAD_DIM
    last_head = (j * GATE_COLS + GATE_COLS - 1) // RNN_HEAD_DIM
    start = min((first_head * RNN_HEAD_DIM) // V7X_LANES * V7X_LANES, D_RNN - GATE_WIN)
    assert start <= first_head * RNN_HEAD_DIM
    assert (last_head + 1) * RNN_HEAD_DIM <= start + GATE_WIN
    return start


GATE_STARTS = tuple(_gate_window_start(j) for j in range(N_GATE_GROUPS))

REST_PIECE = 2 * V7X_MXU_DIM
N_REST_PIECES = (D_IN - OFF_B_X) // REST_PIECE
REST_B_GATE = OFF_B_GATE - OFF_B_X
REST_G_A = OFF_G_A - OFF_B_X
REST_G_B = OFF_G_B - OFF_B_X
REST_END = D_IN - OFF_B_X
assert N_REST_PIECES * REST_PIECE == REST_END

BF16 = jnp.bfloat16
F32 = jnp.float32


def _dot(a, b):
    return jnp.dot(a, b, preferred_element_type=F32)


def _layer_norm(v, g, b):
    mu = jnp.mean(v, axis=-1, keepdims=True)
    c = v - mu
    var = jnp.mean(c * c, axis=-1, keepdims=True)
    return c * lax.rsqrt(var + LN_EPS) * g + b


def _softplus(v):
    return jnp.maximum(v, 0.0) + jnp.log1p(jnp.exp(-jnp.abs(v)))


def _log2(n):
    assert n > 0 and n & (n - 1) == 0, n
    return n.bit_length() - 1


def _row_permutation(m, inner_src, inner_dst):
    i = lax.broadcasted_iota(jnp.int32, (m, m), 0)
    j = lax.broadcasted_iota(jnp.int32, (m, m), 1)
    src = ((i & (inner_dst - 1)) << _log2(inner_src)) + (i >> _log2(inner_dst))
    return jnp.where(j == src, 1.0, 0.0).astype(BF16)


def _mixer_kernel(x_ref, ctxa_ref, ctxb_ref, h0_ref,
                  w_in_ref, w_dwa_ref, b_dwa_ref, lnag_ref, lnab_ref, w_pa_ref,
                  w_dwb_ref, b_dwb_ref, w_gate_ref, b_ra_ref, b_rx_ref, lam_ref,
                  w_pb_ref, w_out_ref, ln1g_ref, ln1b_ref,
                  x1_ref, newa_ref, newb_ref, newh_ref,
                  ua_buf, ub_halo, h_scr, ca_scr, cbb_scr, a_scr, b_scr, hs_scr, rest_scr,
                  *, n_seq, t_chunk, n_steps, chunks_per_step, reset_first, batch_major):
    s = pl.program_id(1)
    m = n_seq * t_chunk
    ha = HALO_A * n_seq
    hb = HALO_B * n_seq

    @pl.when(s == 0)
    def _load_state():
        ua_buf[0:ha, :] = ctxa_ref[...].reshape(ha, D_CONV)
        ub_halo[...] = ctxb_ref[...].reshape(hb, D_RNN)
        h_scr[...] = h0_ref[...]

    def chunk_rows(ref, c):
        if batch_major:
            return ref.at[:, c * t_chunk:(c + 1) * t_chunk, :]
        return ref.at[c * t_chunk:(c + 1) * t_chunk, :, :]

    def one_chunk(c):
        x = chunk_rows(x_ref, c)[...].reshape(m, D_MODEL)
        xb = x.astype(BF16)
        if batch_major:
            xb = _dot(_row_permutation(m, t_chunk, n_seq), xb).astype(BF16)

        def rest_cols(c0, c1):
            parts = []
            while c0 < c1:
                p, off = divmod(c0, REST_PIECE)
                width = min(c1 - c0, REST_PIECE - off)
                parts.append(rest_scr[p, :, off:off + width])
                c0 += width
            return parts[0] if len(parts) == 1 else jnp.concatenate(parts, axis=1)

        def glu_rows(r0, r1):
            z = _dot(xb[r0:r1], w_in_ref[:, OFF_A_VAL:OFF_B_X])
            ua_buf[ha + r0:ha + r1, :] = z[:, :D_CONV] * jax.nn.sigmoid(z[:, D_CONV:])

        def conv_rows(r0):
            loaded = {}

            def context(row):
                if row not in loaded:
                    loaded[row] = ua_buf[row:row + V7X_SUBLANES, :]
                return loaded[row]

            groups = range(r0, r0 + CONV_ROWS, V7X_SUBLANES)
            accs = {g: jnp.broadcast_to(b_dwa_ref[...], (V7X_SUBLANES, D_CONV)) for g in groups}
            for k in range(CONV_K):
                w_k = jnp.concatenate([w_dwa_ref[k, :, CONV_TAP_ROT:],
                                       w_dwa_ref[k, :, :CONV_TAP_ROT]], axis=1)
                for g in groups:
                    accs[g] = accs[g] + w_k * context(g + k * n_seq)
            y = _layer_norm(jnp.concatenate([accs[g] for g in groups], axis=0),
                            lnag_ref[...], lnab_ref[...])
            ca_scr[r0:r0 + CONV_ROWS, :] = (y * jax.nn.sigmoid(y)).astype(BF16)

        glu_block = max(m // GLU_BLOCKS, CONV_ROWS)
        for r0 in range(0, m, glu_block):
            glu_rows(r0, r0 + glu_block)
        for p in range(N_REST_PIECES):
            c0 = OFF_B_X + p * REST_PIECE
            rest_scr[p] = _dot(xb, w_in_ref[:, c0:c0 + REST_PIECE])
        for r0 in range(0, m, CONV_ROWS):
            conv_rows(r0)

        y_a = _dot(ca_scr[...], w_pa_ref[...])
        mix_in = jax.nn.sigmoid(rest_cols(REST_G_A, REST_G_B)) * y_a

        for c0 in range(0, D_RNN, GATE_COLS):
            c1 = c0 + GATE_COLS
            ub = jnp.concatenate([ub_halo[:, c0:c1], rest_cols(c0, c1)], axis=0)
            cb = jnp.broadcast_to(b_dwb_ref[:, c0:c1], (m, GATE_COLS))
            for k in range(RNN_CONV_K):
                cb = cb + w_dwb_ref[k:k + 1, c0:c1] * ub[k * n_seq:k * n_seq + m]
            b_scr[:, c0:c1] = cb
            cbb_scr[:, c0:c1] = cb.astype(BF16)
            ub_halo[:, c0:c1] = ub[m:m + hb]

        log_a_scale = -RG_C * _softplus(-lam_ref[...])
        if reset_first:
            row = lax.broadcasted_iota(jnp.int32, (m, 1), 0)
            first = jnp.logical_and(row < n_seq, jnp.logical_and(s == 0, c == 0))
        for j in range(N_GATE_GROUPS):
            c0, c1 = j * GATE_COLS, (j + 1) * GATE_COLS
            gates = _dot(cbb_scr[:, GATE_STARTS[j]:GATE_STARTS[j] + GATE_WIN], w_gate_ref[j])
            r = jax.nn.sigmoid(gates[:, :GATE_COLS] + b_ra_ref[:, c0:c1])
            ig = jax.nn.sigmoid(gates[:, GATE_COLS:] + b_rx_ref[:, c0:c1])
            a = jnp.exp(log_a_scale[:, c0:c1] * r)
            mult = jnp.sqrt(1.0 - a * a)
            if reset_first:
                a = jnp.where(first, 0.0, a)
                mult = jnp.where(first, 1.0, mult)
            a_scr[:, c0:c1] = a
            b_scr[:, c0:c1] = mult * ig * b_scr[:, c0:c1]

        def scan_body(t, h):
            r0 = pl.multiple_of(t * n_seq, n_seq)
            h = a_scr[pl.ds(r0, n_seq), :] * h + b_scr[pl.ds(r0, n_seq), :]
            hs_scr[pl.ds(r0, n_seq), :] = h
            return h

        h_last = lax.fori_loop(0, t_chunk, scan_body, h_scr[...], unroll=True)
        h_scr[...] = h_last

        yb_in = (hs_scr[...] * jax.nn.gelu(rest_cols(REST_B_GATE, REST_G_A))).astype(BF16)
        y_b = _dot(yb_in, w_pb_ref[...])
        mix_in = (mix_in + jax.nn.sigmoid(rest_cols(REST_G_B, REST_END)) * y_b).astype(BF16)
        if batch_major:
            mix_in = _dot(_row_permutation(m, n_seq, t_chunk), mix_in).astype(BF16)

        mix = _dot(mix_in, w_out_ref[...])
        x1 = _layer_norm(DN_ALPHA * x + mix, ln1g_ref[...], ln1b_ref[...])
        x1_rows = chunk_rows(x1_ref, c)
        x1_rows[...] = x1.reshape(x1_rows.shape)
        return h_last

    for c in range(chunks_per_step):
        h_last = one_chunk(c)
        if c < chunks_per_step - 1:
            ua_buf[0:ha, :] = ua_buf[m:m + ha, :]

    @pl.when(s == n_steps - 1)
    def _emit_state():
        newa_ref[...] = ua_buf[m:m + ha, :].reshape(HALO_A, n_seq, D_CONV)
        newb_ref[...] = ub_halo[...].reshape(HALO_B, n_seq, D_RNN)
        newh_ref[...] = h_last

    if n_steps > 1:
        @pl.when(s < n_steps - 1)
        def _carry():
            ua_buf[0:ha, :] = ua_buf[m:m + ha, :]


def _vmem_full():
    return pl.BlockSpec(memory_space=pltpu.VMEM)


def _mixer(x, ctxa_t, ctxb_t, h0, weights, *, n_seq, t_chunk, reset_first, batch_major,
           chunks_per_step=1):
    if batch_major:
        n_total, t_total, _ = x.shape
        x_block = (n_seq, chunks_per_step * t_chunk, D_MODEL)
        x_index = lambda g, s: (g, s, 0)
    else:
        t_total, n_total, _ = x.shape
        x_block = (chunks_per_step * t_chunk, n_seq, D_MODEL)
        x_index = lambda g, s: (s, g, 0)
    n_groups = n_total // n_seq
    n_chunks = t_total // t_chunk
    n_steps = n_chunks // chunks_per_step
    assert n_groups * n_seq == n_total and n_steps * chunks_per_step * t_chunk == t_total
    assert n_seq % V7X_SUBLANES == 0 and (n_seq * t_chunk) % CONV_ROWS == 0
    assert n_chunks == 1 or t_chunk >= HALO_A
    assert not batch_major or t_chunk % 16 == 0
    m = n_seq * t_chunk

    kernel = functools.partial(_mixer_kernel, n_seq=n_seq, t_chunk=t_chunk, n_steps=n_steps,
                               chunks_per_step=chunks_per_step, reset_first=reset_first,
                               batch_major=batch_major)
    state_specs = [
        pl.BlockSpec((HALO_A, n_seq, D_CONV), lambda g, s: (0, g, 0)),
        pl.BlockSpec((HALO_B, n_seq, D_RNN), lambda g, s: (0, g, 0)),
        pl.BlockSpec((n_seq, D_RNN), lambda g, s: (g, 0)),
    ]
    in_specs = [pl.BlockSpec(x_block, x_index)] + state_specs + [_vmem_full() for _ in weights]
    out_specs = [pl.BlockSpec(x_block, x_index)] + state_specs
    out_shape = [
        jax.ShapeDtypeStruct(x.shape, F32),
        jax.ShapeDtypeStruct((HALO_A, n_total, D_CONV), F32),
        jax.ShapeDtypeStruct((HALO_B, n_total, D_RNN), F32),
        jax.ShapeDtypeStruct((n_total, D_RNN), F32),
    ]
    scratch = [
        pltpu.VMEM(((HALO_A + t_chunk) * n_seq, D_CONV), F32),
        pltpu.VMEM((HALO_B * n_seq, D_RNN), F32),
        pltpu.VMEM((n_seq, D_RNN), F32),
        pltpu.VMEM((m, D_CONV), BF16),
        pltpu.VMEM((m, D_RNN), BF16),
        pltpu.VMEM((m, D_RNN), F32),
        pltpu.VMEM((m, D_RNN), F32),
        pltpu.VMEM((m, D_RNN), F32),
        pltpu.VMEM((N_REST_PIECES, m, REST_PIECE), F32),
    ]
    return pl.pallas_call(
        kernel,
        grid=(n_groups, n_steps),
        in_specs=in_specs,
        out_specs=out_specs,
        out_shape=out_shape,
        scratch_shapes=scratch,
        compiler_params=pltpu.CompilerParams(
            dimension_semantics=("arbitrary", "arbitrary"),
            vmem_limit_bytes=V7X_VMEM_LIMIT_BYTES),
        name="mixer",
    )(x, ctxa_t, ctxb_t, h0, *weights)


def _ffn_kernel(x1_ref, p_ref, w1_ref, w2_ref, wg_ref, wp_ref, ln2g_ref, ln2b_ref, y_ref):
    n_rows = x1_ref.shape[0]
    block = max(n_rows // FFN_ROW_BLOCKS, 16)
    for r0 in range(0, n_rows, block):
        x1 = x1_ref[r0:r0 + block, :]
        x1b = x1.astype(BF16)
        hid = jnp.maximum(_dot(x1b, w1_ref[...]), 0.0)
        ff = _dot((hid * hid).astype(BF16), w2_ref[...])
        ple = (jax.nn.sigmoid(_dot(x1b, wg_ref[...]))
               * _dot(p_ref[r0:r0 + block, :].astype(BF16), wp_ref[...]))
        y_ref[r0:r0 + block, :] = _layer_norm(DN_ALPHA * x1 + ff + ple,
                                              ln2g_ref[...], ln2b_ref[...])


def _ffn(x1, p, weights):
    n_rows = x1.shape[0]
    rows = min(FFN_ROWS, n_rows)
    assert n_rows % rows == 0
    in_specs = [
        pl.BlockSpec((rows, D_MODEL), lambda i: (i, 0)),
        pl.BlockSpec((rows, D_PLE), lambda i: (i, 0)),
    ] + [_vmem_full() for _ in weights]
    return pl.pallas_call(
        _ffn_kernel,
        grid=(n_rows // rows,),
        in_specs=in_specs,
        out_specs=pl.BlockSpec((rows, D_MODEL), lambda i: (i, 0)),
        out_shape=jax.ShapeDtypeStruct((n_rows, D_MODEL), F32),
        compiler_params=pltpu.CompilerParams(
            dimension_semantics=("arbitrary",),
            vmem_limit_bytes=V7X_VMEM_LIMIT_BYTES),
        name="ffn",
    )(x1, p, *weights)


def _block_diag(w):
    h, i, j = w.shape
    eye = jnp.eye(h, dtype=w.dtype)
    return (w[:, :, None, :] * eye[:, None, :, None]).reshape(h * i, h * j)


def _gate_windows(w_rg_a, w_rg_x):
    dense_a, dense_x = _block_diag(w_rg_a.astype(BF16)), _block_diag(w_rg_x.astype(BF16))
    groups = []
    for j, start in enumerate(GATE_STARTS):
        rows = slice(start, start + GATE_WIN)
        cols = slice(j * GATE_COLS, (j + 1) * GATE_COLS)
        groups.append(jnp.concatenate([dense_a[rows, cols], dense_x[rows, cols]], axis=1))
    return jnp.stack(groups)


def _time_major(v):
    return jnp.transpose(v, (1, 0, 2))


def kernel(x_prompt, x_sample, state_conv_a, state_conv_b, state_h, p_prompt, p_sample, w_in, w_dw_a, b_dw_a, ln_a_g, ln_a_b, w_proj_a, w_dw_b, b_dw_b, w_rg_a, b_rg_a, w_rg_x, b_rg_x, rg_lam, w_proj_b, w_out, ln1_g, ln1_b, w_ff1, w_ff2, w_ple_gate, w_ple_proj, ln2_g, ln2_b):
    assert w_in.shape[0] == DEPTH == 1
    row = lambda v: v[0].reshape(1, -1)
    mixer_w = (
        w_in[0].astype(BF16),
        jnp.broadcast_to(jnp.roll(w_dw_a[0], CONV_TAP_ROT, axis=1)[:, None, :],
                         (CONV_K, V7X_SUBLANES, D_CONV)),
        row(b_dw_a), row(ln_a_g), row(ln_a_b),
        w_proj_a[0].astype(BF16),
        w_dw_b[0], row(b_dw_b),
        _gate_windows(w_rg_a[0], w_rg_x[0]), row(b_rg_a), row(b_rg_x), row(rg_lam),
        w_proj_b[0].astype(BF16), w_out[0].astype(BF16), row(ln1_g), row(ln1_b),
    )
    ffn_w = (
        w_ff1[0].astype(BF16), w_ff2[0].astype(BF16), w_ple_gate[0].astype(BF16),
        w_ple_proj[0].astype(BF16), row(ln2_g), row(ln2_b),
    )

    bp, tp, _ = x_prompt.shape
    x1_p, na_p, nb_p, nh_p = _mixer(
        x_prompt, jnp.zeros((HALO_A, bp, D_CONV), F32), jnp.zeros((HALO_B, bp, D_RNN), F32),
        jnp.zeros((bp, D_RNN), F32), mixer_w,
        n_seq=bp, t_chunk=32, reset_first=True, batch_major=True, chunks_per_step=1)
    y_p = _ffn(x1_p.reshape(bp * tp, D_MODEL), p_prompt[0].reshape(bp * tp, D_PLE), ffn_w)

    bs, ts, _ = x_sample.shape
    x1_s, na_s, nb_s, nh_s = _mixer(
        _time_major(x_sample), _time_major(state_conv_a[0]), _time_major(state_conv_b[0]),
        state_h[0], mixer_w, n_seq=32, t_chunk=ts, reset_first=False, batch_major=False)
    y_s = _ffn(x1_s.reshape(ts * bs, D_MODEL), _time_major(p_sample[0]).reshape(ts * bs, D_PLE),
               ffn_w)

    return (y_p.reshape(bp, tp, D_MODEL), _time_major(y_s.reshape(ts, bs, D_MODEL)),
            _time_major(na_p)[None], _time_major(nb_p)[None], nh_p[None],
            _time_major(na_s)[None], _time_major(nb_s)[None], nh_s[None])
```
